```python
import jax
import jax.numpy as jnp
from jax import lax
import numpy as np

D_MODEL = 1024
BATCH = 4
SEQ = 8192
DEPTH = 4

HG_HEADS = 4
HG_DK = 128
HG_DV = 128
HG_CHUNK = 16
LOG_TINY = -87.0
CV_W = 512
CV_K = 31
SB_HEADS = 8
SB_DH = 64
SB_W = SB_HEADS * SB_DH
SB_BLOCK = 128
RT_HEADS = 4
RT_DH = 128
RT_W = RT_HEADS * RT_DH
RT_CHUNK = 128
BRANCH_W = (HG_HEADS * HG_DV, CV_W, SB_W, RT_W)
N_BRANCH = 4
MIX_W = HG_HEADS * HG_DV + CV_W + SB_W + RT_W
IN_SPLIT = (HG_HEADS * HG_DK, HG_HEADS * HG_DK, HG_HEADS * HG_DV, HG_HEADS * HG_DV, 2 * CV_W,
            SB_W, SB_W, SB_W, RT_W, RT_W, RT_W, RT_W)
IN_COLS = 4 * HG_HEADS * HG_DK + 2 * CV_W + 3 * SB_W + 4 * RT_W
N_GROUPS = 4
EXP_PER_GROUP = 8
N_EXPERTS = N_GROUPS * EXP_PER_GROUP
TOP_K = 2
D_EXPERT = 512
MOE_BLOCK = 128
EPS = 1e-6

kernel_name = 'hybrid_gated_hgrn2_conv_stickbreak_retention_hmoe'


def head_rms(t, g):
    tf = t.astype(jnp.float32)
    return tf * lax.rsqrt(jnp.mean(tf * tf, axis=-1, keepdims=True) + EPS) * g.astype(jnp.float32)


def rms_norm(x, g):
    return head_rms(x, g).astype(x.dtype)


def layer_norm(x, g, b):
    xf = x.astype(jnp.float32)
    xc = xf - jnp.mean(xf, axis=-1, keepdims=True)
    return xc * lax.rsqrt(jnp.mean(xc * xc, axis=-1, keepdims=True) + EPS) * g.astype(jnp.float32) + b.astype(jnp.float32)


def hgrn2_mixer(q, f_pre, i_in, g_out, lb, norm_g):
    Bn, S, _ = q.shape
    N, C = S // HG_CHUNK, HG_CHUNK

    def heads(t, d):
        return t.astype(jnp.float32).reshape(Bn, N, C, HG_HEADS, d).transpose(1, 0, 3, 2, 4)

    qh = jax.nn.silu(heads(q, HG_DK))
    fx = heads(f_pre, HG_DK)
    vh = heads(i_in, HG_DV)
    lb_h = lb.astype(jnp.float32).reshape(HG_HEADS, 1, HG_DK)
    log_lb = jnp.maximum(jnp.log(jnp.maximum(lb_h, 1e-30)), LOG_TINY)
    log_f = jnp.logaddexp(log_lb, jnp.log1p(-lb_h) + jax.nn.log_sigmoid(fx))
    kh = (1.0 - lb_h) * jax.nn.sigmoid(-fx)
    b = jnp.cumsum(log_f, axis=3)
    b_last = b[:, :, :, -1:, :]
    causal = jnp.tril(jnp.ones((C, C), dtype=bool))[:, :, None]
    diff = b[:, :, :, :, None, :] - b[:, :, :, None, :, :]
    decay = jnp.where(causal, jnp.exp(jnp.where(causal, diff, 0.0)), 0.0)
    scores = jnp.einsum('nbhtd,nbhsd,nbhtsd->nbhts', qh, kh, decay)
    intra = jnp.einsum('nbhts,nbhsv->nbhtv', scores, vh)
    q_dec = qh * jnp.exp(b)
    k_state = kh * jnp.exp(b_last - b)
    chunk_decay = jnp.exp(b_last[:, :, :, 0, :])

    def step(state, xs):
        q_c, k_c, v_c, dec = xs
        o = jnp.einsum('bhtd,bhdv->bhtv', q_c, state)
        state = dec[..., None] * state + jnp.einsum('bhsd,bhsv->bhdv', k_c, v_c)
        return state, o

    s0 = jnp.zeros((Bn, HG_HEADS, HG_DK, HG_DV), jnp.float32)
    _, inter = lax.scan(step, s0, (q_dec, k_state, vh, chunk_decay))
    o = (intra + inter).transpose(1, 0, 3, 2, 4).reshape(Bn, S, HG_HEADS, HG_DV)
    o = head_rms(o, norm_g.reshape(HG_HEADS, HG_DV)).reshape(Bn, S, HG_HEADS * HG_DV)
    return o * jax.nn.silu(g_out.astype(jnp.float32))


def conformer_conv(u, conv_w, conv_b, ln_g, ln_b):
    a, gate = jnp.split(u, 2, axis=-1)
    z = a * jax.nn.sigmoid(gate)
    z = lax.conv_general_dilated(z, conv_w[:, None, :].astype(z.dtype), window_strides=(1,),
                                 padding=((CV_K - 1, 0),), dimension_numbers=('NWC', 'WIO', 'NWC'),
                                 feature_group_count=CV_W)
    z = z + conv_b
    return jax.nn.silu(layer_norm(z, ln_g, ln_b))


def stick_breaking_attention(q, k, v, qn_g, kn_g):
    Bn, S, _ = q.shape
    nb = S // SB_BLOCK

    def heads(t):
        return t.reshape(Bn, S, SB_HEADS, SB_DH)

    qh = (head_rms(heads(q), qn_g) * (SB_DH ** -0.5)).transpose(0, 2, 1, 3)
    kh = head_rms(heads(k), kn_g).transpose(0, 2, 1, 3)
    vh = heads(v).astype(jnp.float32).transpose(0, 2, 1, 3)
    q_blocks = qh.reshape(Bn, SB_HEADS, nb, SB_BLOCK, SB_DH).transpose(2, 0, 1, 3, 4)
    key_pos = jnp.arange(S)

    def block(args):
        q_b, start = args
        z = jnp.einsum('bhtd,bhsd->bhts', q_b, kh)
        past = key_pos[None, :] < (start + jnp.arange(SB_BLOCK))[:, None]
        log_keep = jnp.where(past, jax.nn.log_sigmoid(-z), 0.0)
        log_between = lax.cumsum(log_keep, axis=3, reverse=True) - log_keep
        w = jnp.where(past, jnp.exp(jax.nn.log_sigmoid(z) + log_between), 0.0)
        return jnp.einsum('bhts,bhsd->bhtd', w, vh)

    o = lax.map(block, (q_blocks, jnp.arange(nb) * SB_BLOCK))
    return o.transpose(1, 0, 3, 2, 4).reshape(Bn, S, SB_W)


def retention(q, k, v, g, norm_g):
    Bn, S, _ = q.shape
    N, C = S // RT_CHUNK, RT_CHUNK

    def heads(t):
        return t.astype(jnp.float32).reshape(Bn, N, C, RT_HEADS, RT_DH).transpose(1, 0, 3, 2, 4)

    qh, kh, vh = heads(q), heads(k) * (RT_DH ** -0.5), heads(v)
    log_gamma = jnp.log1p(-jnp.exp2(-5.0 - jnp.arange(RT_HEADS, dtype=jnp.float32)))[:, None, None]
    pos = jnp.arange(C, dtype=jnp.float32)[:, None]
    rel = pos - pos.T
    decay_mat = jnp.where(rel >= 0, jnp.exp(log_gamma * jnp.maximum(rel, 0.0)), 0.0)
    intra = jnp.einsum('nbhts,nbhsv->nbhtv', jnp.einsum('nbhtd,nbhsd->nbhts', qh, kh) * decay_mat, vh)
    q_dec = qh * jnp.exp(log_gamma * (pos + 1.0))
    k_dec = kh * jnp.exp(log_gamma * (C - 1.0 - pos))
    chunk_decay = jnp.exp(log_gamma * C)

    def step(state, xs):
        q_c, k_c, v_c = xs
        o = jnp.einsum('bhtd,bhdv->bhtv', q_c, state)
        state = chunk_decay * state + jnp.einsum('bhsd,bhsv->bhdv', k_c, v_c)
        return state, o

    s0 = jnp.zeros((Bn, RT_HEADS, RT_DH, RT_DH), jnp.float32)
    _, inter = lax.scan(step, s0, (q_dec, k_dec, vh))
    o = (intra + inter).transpose(1, 0, 3, 2, 4).reshape(Bn, S, RT_HEADS, RT_DH)
    o = head_rms(o, norm_g.reshape(RT_HEADS, RT_DH)).reshape(Bn, S, RT_W)
    return o * jax.nn.silu(g.astype(jnp.float32))


def hybrid_mixer(h, w_in, lb, hg_g, conv_w, conv_b, ln_g, ln_b, qn_g, kn_g, rt_g, w_branch, w_gate, b_gate, w_out):
    offs = [0]
    for w in IN_SPLIT:
        offs.append(offs[-1] + w)

    def proj(j):
        return h @ w_in[:, offs[j]:offs[j + 1]]

    o_hg = hgrn2_mixer(proj(0), proj(1), proj(2), proj(3), lb, hg_g)
    o_cv = conformer_conv(proj(4), conv_w, conv_b, ln_g, ln_b)
    o_sb = stick_breaking_attention(proj(5), proj(6), proj(7), qn_g, kn_g)
    o_rt = retention(proj(8), proj(9), proj(10), proj(11), rt_g)
    merged = jnp.zeros_like(h)
    r = 0
    for n, o in enumerate((o_hg, o_cv, o_sb, o_rt)):
        bw = BRANCH_W[n]
        y = o.astype(h.dtype) @ w_branch[r:r + bw]
        r += bw
        gate = jax.nn.sigmoid(h @ w_gate[:, n * D_MODEL:(n + 1) * D_MODEL] + b_gate[n * D_MODEL:(n + 1) * D_MODEL])
        merged = merged + gate * y
    return merged @ w_out


def hier_moe(h, wg, bg, we, be, w1, w3, w2):
    Bn, S, D = h.shape
    T = Bn * S
    hf = h.reshape(T, D)
    g_prob = jax.nn.softmax((hf @ wg + bg).astype(jnp.float32), axis=-1)
    g_p, g_idx = lax.top_k(g_prob, 1)
    e_logits = (hf @ we + be).astype(jnp.float32).reshape(T, N_GROUPS, EXP_PER_GROUP)
    e_in_group = jnp.take_along_axis(e_logits, g_idx[:, :, None], axis=1)[:, 0]
    top_v, top_i = lax.top_k(e_in_group, TOP_K)
    gate = g_p * jax.nn.softmax(top_v, axis=-1)
    expert = g_idx * EXP_PER_GROUP + top_i
    A = T * TOP_K
    flat_e = expert.reshape(A)
    flat_t = jnp.repeat(jnp.arange(T, dtype=jnp.int32), TOP_K)
    flat_g = gate.reshape(A)
    order = jnp.argsort(flat_e)
    e_sorted = flat_e[order]
    counts = jnp.bincount(flat_e, length=N_EXPERTS)
    padded = (counts + MOE_BLOCK - 1) // MOE_BLOCK * MOE_BLOCK
    pad_end = jnp.cumsum(padded)
    pad_start = pad_end - padded
    grp_start = jnp.cumsum(counts) - counts
    dest = pad_start[e_sorted] + jnp.arange(A, dtype=jnp.int32) - grp_start[e_sorted]
    P = (A + N_EXPERTS * (MOE_BLOCK - 1) + MOE_BLOCK - 1) // MOE_BLOCK * MOE_BLOCK
    NB = P // MOE_BLOCK
    row_tok = jnp.full((P,), T, jnp.int32).at[dest].set(flat_t[order])
    row_gate = jnp.zeros((P,), jnp.float32).at[dest].set(flat_g[order])
    blk_expert = jnp.minimum(jnp.searchsorted(pad_end, jnp.arange(NB) * MOE_BLOCK, side='right'), N_EXPERTS - 1)
    x_pad = jnp.concatenate([hf, jnp.zeros((1, D), hf.dtype)], axis=0)
    xb = x_pad[row_tok].reshape(NB, MOE_BLOCK, D)

    def expert_block(args):
        xe, e = args
        return (jax.nn.silu(xe @ w1[e]) * (xe @ w3[e])) @ w2[e]

    yb = lax.map(expert_block, (xb, blk_expert)).reshape(P, D)
    yb = yb * row_gate[:, None].astype(yb.dtype)
    out = jax.ops.segment_sum(yb, row_tok, num_segments=T + 1)[:T]
    return out.reshape(Bn, S, D)


def setup_inputs(seed: int = 0) -> dict:
    key = jax.random.key(seed)
    ks = jax.random.split(key, 27)
    n = lambda k, s, sc: jax.random.normal(k, s, jnp.float32) * sc
    L, D = DEPTH, D_MODEL
    return {
        'x': n(ks[0], (BATCH, SEQ, D), 1.0),
        'c': n(ks[1], (BATCH, D), 1.0),
        'ada_w': n(ks[2], (L, D, 6 * D), 0.5 * D ** -0.5),
        'ada_b': n(ks[3], (L, 6 * D), 0.01),
        'norm1_g': 1.0 + n(ks[4], (L, D), 0.02),
        'norm2_g': 1.0 + n(ks[5], (L, D), 0.02),
        'w_in': n(ks[6], (L, D, IN_COLS), D ** -0.5),
        'hgrn_lb': n(ks[7], (L, HG_HEADS * HG_DK), 0.1),
        'hgrn_norm_g': 1.0 + n(ks[8], (L, HG_HEADS * HG_DV), 0.02),
        'conv_w': n(ks[9], (L, CV_K, CV_W), CV_K ** -0.5),
        'conv_b': n(ks[10], (L, CV_W), 0.01),
        'conv_ln_g': 1.0 + n(ks[11], (L, CV_W), 0.02),
        'conv_ln_b': n(ks[12], (L, CV_W), 0.01),
        'sb_qnorm_g': 1.0 + n(ks[13], (L, SB_DH), 0.02),
        'sb_knorm_g': 1.0 + n(ks[14], (L, SB_DH), 0.02),
        'ret_norm_g': 1.0 + n(ks[15], (L, RT_W), 0.02),
        'w_branch': n(ks[16], (L, MIX_W, D), 512 ** -0.5),
        'w_gate': n(ks[17], (L, D, N_BRANCH * D), D ** -0.5),
        'b_gate': n(ks[18], (L, N_BRANCH * D), 0.01),
        'w_out': n(ks[19], (L, D, D), D ** -0.5),
        'router_group_w': n(ks[20], (L, D, N_GROUPS), D ** -0.5),
        'router_group_b': n(ks[21], (L, N_GROUPS), 0.01),
        'router_expert_w': n(ks[22], (L, D, N_EXPERTS), D ** -0.5),
        'router_expert_b': n(ks[23], (L, N_EXPERTS), 0.01),
        'expert_w1': n(ks[24], (L, N_EXPERTS, D, D_EXPERT), D ** -0.5),
        'expert_w3': n(ks[25], (L, N_EXPERTS, D, D_EXPERT), D ** -0.5),
        'expert_w2': n(ks[26], (L, N_EXPERTS, D_EXPERT, D), D_EXPERT ** -0.5),
    }


def reference(x, c, ada_w, ada_b, norm1_g, norm2_g, w_in, hgrn_lb, hgrn_norm_g, conv_w, conv_b, conv_ln_g,
              conv_ln_b, sb_qnorm_g, sb_knorm_g, ret_norm_g, w_branch, w_gate, b_gate, w_out, router_group_w,
              router_group_b, router_expert_w, router_expert_b, expert_w1, expert_w3, expert_w2):
    sm = jax.nn.softmax(hgrn_lb.astype(jnp.float32), axis=0)
    lower_bounds = jnp.cumsum(sm, axis=0) - sm[0]
    mods = jnp.einsum('bd,lde->lbe', jax.nn.silu(c), ada_w) + ada_b[:, None, :]
    for l in range(DEPTH):
        sh1, sc1, g1, sh2, sc2, g2 = jnp.split(mods[l][:, None, :], 6, axis=-1)
        h = rms_norm(x, norm1_g[l]) * (1.0 + sc1) + sh1
        x = x + g1 * hybrid_mixer(h, w_in[l], lower_bounds[l], hgrn_norm_g[l], conv_w[l], conv_b[l], conv_ln_g[l],
                                  conv_ln_b[l], sb_qnorm_g[l], sb_knorm_g[l], ret_norm_g[l], w_branch[l], w_gate[l],
                                  b_gate[l], w_out[l])
        h = rms_norm(x, norm2_g[l]) * (1.0 + sc2) + sh2
        x = x + g2 * hier_moe(h, router_group_w[l], router_group_b[l], router_expert_w[l], router_expert_b[l],
                              expert_w1[l], expert_w3[l], expert_w2[l])
    return x
```

```python
import functools
import math

import jax
import jax.numpy as jnp
from jax import lax
from jax.experimental import pallas as pl
from jax.experimental.pallas import tpu as pltpu

F32 = jnp.float32
BF16 = jnp.bfloat16
I32 = jnp.int32

LANES = 128
VMEM_LIMIT = 56 * 1024 * 1024

D_MODEL = 1024
EPS = 1e-6
LOG_TINY = -87.0
HG_HEADS, HG_DK, HG_CHUNK = 4, 128, 16
CV_W, CV_K = 512, 31
CV_HALO = 32
SB_DH = 64
RT_HEADS, RT_DH, RT_CHUNK = 4, 128, 128
N_BRANCH = 4
BRANCH_W = 512
N_GROUPS, EXP_PER_GROUP, N_EXPERTS = 4, 8, 32
D_EXPERT = 512
MOE_ROWS = 256

G_HQ, G_HF, G_HI, G_HG = 0, 4, 8, 12
G_CA, G_CG = 16, 20
G_SQ, G_SK, G_SV = 24, 28, 32
G_RQ, G_RK, G_RV, G_RG = 36, 40, 44, 48
N_COL_GROUPS = 52


def _dot(a, b):
    return jnp.dot(a, b, preferred_element_type=F32)


def _dot_nt(a, b):
    return lax.dot_general(a, b, (((1,), (1,)), ((), ())), preferred_element_type=F32)


def _dot_tn(a, b):
    return lax.dot_general(a, b, (((0,), (0,)), ((), ())), preferred_element_type=F32)


def _split3(x):
    hi = x.astype(BF16)
    r1 = x - hi.astype(F32)
    mid = r1.astype(BF16)
    lo = (r1 - mid.astype(F32)).astype(BF16)
    return hi, mid, lo


def _rms_mod(x, g, sc, sh):
    ms = jnp.mean(x * x, axis=-1, keepdims=True)
    return x * lax.rsqrt(ms + EPS) * g * (1.0 + sc) + sh


def _silu(x):
    return x * jax.nn.sigmoid(x)


def _softplus(z):
    return jnp.maximum(z, 0.0) + jnp.log1p(jnp.exp(-jnp.abs(z)))


def _params(*sem):
    return pltpu.CompilerParams(dimension_semantics=sem, vmem_limit_bytes=VMEM_LIMIT)


def _mods_kernel(c_ref, w_ref, b_ref, o_ref):
    c = c_ref[...]
    sc = _silu(c)
    w = w_ref[0]
    acc = None
    for cp in _split3(sc):
        for wp in _split3(w)[:2]:
            t = _dot(cp, wp)
            acc = t if acc is None else acc + t
    o_ref[0] = acc + b_ref[0]


def _mods(c, ada_w, ada_b):
    L, D, E = ada_w.shape
    B = c.shape[0]
    rows = 8
    tn = 1536
    cp = jnp.zeros((rows, D), F32).at[:B].set(c)
    out = pl.pallas_call(
        _mods_kernel,
        grid=(L, E // tn),
        in_specs=[pl.BlockSpec((rows, D), lambda l, j: (0, 0)),
                  pl.BlockSpec((1, D, tn), lambda l, j: (l, 0, j)),
                  pl.BlockSpec((1, 1, tn), lambda l, j: (l, 0, j))],
        out_specs=pl.BlockSpec((1, rows, tn), lambda l, j: (l, 0, j)),
        out_shape=jax.ShapeDtypeStruct((L, rows, E), F32),
        compiler_params=_params("parallel", "parallel"),
        name="adaln_mods",
    )(cp, ada_w, ada_b.reshape(L, 1, E))
    return out[:, :B]


def _in_kernel(x_ref, g_ref, sc_ref, sh_ref, w_ref, o_ref, h_scr):
    @pl.when(pl.program_id(1) == 0)
    def _():
        h_scr[...] = _rms_mod(x_ref[...], g_ref[...], sc_ref[0], sh_ref[0]).astype(BF16)

    res = _dot(h_scr[...], w_ref[0])
    for q in range(res.shape[1] // LANES):
        o_ref[q] = res[:, q * LANES:(q + 1) * LANES].astype(BF16)


def _in_proj(x2, g, sc, sh, w_in_b, l, S):
    T, D = x2.shape
    tm, tn = 1024, 512
    cols = w_in_b.shape[2]
    tpb = S // tm
    return pl.pallas_call(
        _in_kernel,
        grid=(T // tm, cols // tn),
        in_specs=[pl.BlockSpec((tm, D), lambda i, j: (i, 0)),
                  pl.BlockSpec((1, D), lambda i, j: (0, 0)),
                  pl.BlockSpec((1, 1, D), lambda i, j: (i // tpb, 0, 0)),
                  pl.BlockSpec((1, 1, D), lambda i, j: (i // tpb, 0, 0)),
                  pl.BlockSpec((1, D, tn), lambda i, j: (l, 0, j))],
        out_specs=pl.BlockSpec((tn // LANES, tm, LANES), lambda i, j: (j, i, 0)),
        out_shape=jax.ShapeDtypeStruct((cols // LANES, T, LANES), BF16),
        scratch_shapes=[pltpu.VMEM((tm, D), BF16)],
        compiler_params=_params("parallel", "arbitrary"),
        name="in_proj",
    )(x2, g, sc, sh, w_in_b)


def _hgrn_kernel(q_ref, f_ref, i_ref, g_ref, lb_ref, ng_ref, o_ref, b_scr, k_scr, q_scr, st_scr, *, ts):
    C = HG_CHUNK

    @pl.when(pl.program_id(1) == 0)
    def _():
        st_scr[...] = jnp.zeros_like(st_scr)

    lb = lb_ref[...]
    log_lb = jnp.maximum(jnp.log(jnp.maximum(lb, 1e-30)), LOG_TINY)
    log_1m = jnp.log1p(-lb)
    one_m = 1.0 - lb
    ng = ng_ref[...]
    rr = lax.broadcasted_iota(I32, (ts, ts), 0)
    cc = lax.broadcasted_iota(I32, (ts, ts), 1)
    shift = C.bit_length() - 1
    tri = jnp.where((jnp.right_shift(rr, shift) == jnp.right_shift(cc, shift)) & (cc <= rr), 1.0, 0.0).astype(BF16)

    for h in range(HG_HEADS):
        sl = slice(h * LANES, (h + 1) * LANES)
        fx = f_ref[h].astype(F32)
        lsig = jnp.minimum(fx, 0.0) - jnp.log1p(jnp.exp(-jnp.abs(fx)))
        a = log_lb[:, sl]
        bb = log_1m[:, sl] + lsig
        log_f = jnp.maximum(a, bb) + jnp.log1p(jnp.exp(-jnp.abs(a - bb)))
        hi, mid, lo = _split3(log_f)
        b_scr[h] = _dot(tri, hi) + _dot(tri, mid) + _dot(tri, lo)
        k_scr[h] = one_m[:, sl] * jax.nn.sigmoid(-fx)
        q_scr[h] = _silu(q_ref[h].astype(F32))

    rows = lax.broadcasted_iota(I32, (C, LANES), 0)

    def chunk(c, carry):
        r0 = pl.multiple_of(c * C, C)
        for h in range(HG_HEADS):
            sl = slice(h * LANES, (h + 1) * LANES)
            b = b_scr[h, pl.ds(r0, C), :]
            q = q_scr[h, pl.ds(r0, C), :]
            k = k_scr[h, pl.ds(r0, C), :]
            v = i_ref[h, pl.ds(r0, C), :].astype(F32)
            b_last = b[C - 1:C, :]
            q_dec = q * jnp.exp(b)
            k_st = k * jnp.exp(b_last - b)
            st = st_scr[h]
            inter = _dot_nt(q_dec.astype(BF16), st.astype(BF16))
            st_scr[h] = st * jnp.exp(b_last) + _dot_tn(v.astype(BF16), k_st.astype(BF16))
            acc = inter
            for s in range(C):
                dec = jnp.exp(jnp.where(rows >= s, b - b[s:s + 1, :], -1e30))
                sc = jnp.sum(q * k[s:s + 1, :] * dec, axis=-1, keepdims=True)
                acc = acc + sc * v[s:s + 1, :]
            o = acc * lax.rsqrt(jnp.mean(acc * acc, axis=-1, keepdims=True) + EPS) * ng[:, sl]
            gg = g_ref[h, pl.ds(r0, C), :].astype(F32)
            o_ref[pl.ds(r0, C), sl] = (o * _silu(gg)).astype(BF16)
        return carry

    lax.fori_loop(0, ts // C, chunk, 0)


def _hgrn(p3, lb, ng, B, S):
    T = B * S
    ts = 128
    nt = S // ts

    def spec(g0):
        return pl.BlockSpec((HG_HEADS, ts, LANES), lambda b, i: (g0 // HG_HEADS, b * nt + i, 0))

    return pl.pallas_call(
        functools.partial(_hgrn_kernel, ts=ts),
        grid=(B, nt),
        in_specs=[spec(G_HQ), spec(G_HF), spec(G_HI), spec(G_HG),
                  pl.BlockSpec((1, HG_HEADS * HG_DK), lambda b, i: (0, 0)),
                  pl.BlockSpec((1, HG_HEADS * HG_DK), lambda b, i: (0, 0))],
        out_specs=pl.BlockSpec((ts, BRANCH_W), lambda b, i: (b * nt + i, 0)),
        out_shape=jax.ShapeDtypeStruct((T, BRANCH_W), BF16),
        scratch_shapes=[pltpu.VMEM((HG_HEADS, ts, LANES), F32),
                        pltpu.VMEM((HG_HEADS, ts, LANES), F32),
                        pltpu.VMEM((HG_HEADS, ts, LANES), F32),
                        pltpu.VMEM((HG_HEADS, HG_DK, HG_DK), F32)],
        compiler_params=_params("parallel", "arbitrary"),
        name="hgrn2",
    )(p3, p3, p3, p3, lb, ng)


def _conv_kernel(a_ref, g_ref, w_ref, cb_ref, lg_ref, lbias_ref, o_ref, z_scr, y_scr, *, ts):
    H = CV_HALO

    @pl.when(pl.program_id(1) == 0)
    def _():
        z_scr[0:H, :] = jnp.zeros((H, CV_W), F32)

    @pl.when(pl.program_id(1) > 0)
    def _():
        z_scr[0:H, :] = z_scr[ts:ts + H, :]

    for q in range(CV_W // LANES):
        sl = slice(q * LANES, (q + 1) * LANES)
        z_scr[H:H + ts, sl] = a_ref[q].astype(F32) * jax.nn.sigmoid(g_ref[q].astype(F32))

    off = H - (CV_K - 1)
    for q in range(CV_W // LANES):
        sl = slice(q * LANES, (q + 1) * LANES)
        acc = jnp.zeros((ts, LANES), F32)
        for j in range(CV_K):
            acc = acc + w_ref[j:j + 1, sl] * z_scr[off + j:off + j + ts, sl]
        y_scr[:, sl] = acc + cb_ref[:, sl]

    y = y_scr[...]
    mean = jnp.mean(y, axis=-1, keepdims=True)
    yc = y - mean
    var = jnp.mean(yc * yc, axis=-1, keepdims=True)
    yn = yc * lax.rsqrt(var + EPS) * lg_ref[...] + lbias_ref[...]
    o_ref[...] = _silu(yn).astype(BF16)


def _conv(p3, conv_w, conv_b, ln_g, ln_b, B, S):
    T = B * S
    ts = 256
    nt = S // ts
    ng = CV_W // LANES

    def spec(g0):
        return pl.BlockSpec((ng, ts, LANES), lambda b, i: (g0 // ng, b * nt + i, 0))

    vec = pl.BlockSpec((1, CV_W), lambda b, i: (0, 0))
    return pl.pallas_call(
        functools.partial(_conv_kernel, ts=ts),
        grid=(B, nt),
        in_specs=[spec(G_CA), spec(G_CG), pl.BlockSpec((CV_K, CV_W), lambda b, i: (0, 0)), vec, vec, vec],
        out_specs=pl.BlockSpec((ts, CV_W), lambda b, i: (b * nt + i, 0)),
        out_shape=jax.ShapeDtypeStruct((T, CV_W), BF16),
        scratch_shapes=[pltpu.VMEM((ts + CV_HALO, CV_W), F32), pltpu.VMEM((ts, CV_W), F32)],
        compiler_params=_params("parallel", "arbitrary"),
        name="conformer_conv",
    )(p3, p3, conv_w, conv_b, ln_g, ln_b)


def _sb_kernel(q_ref, k_ref, v_ref, qg_ref, kg_ref, o_ref, kn_scr, *, S, tq):
    i = pl.program_id(2)
    lane = lax.broadcasted_iota(I32, (1, LANES), 1)
    half0 = lane < SB_DH

    def head_rms(x, g):
        x2 = x * x
        s0 = jnp.sum(jnp.where(half0, x2, 0.0), axis=-1, keepdims=True)
        s1 = jnp.sum(jnp.where(half0, 0.0, x2), axis=-1, keepdims=True)
        ms = jnp.where(half0, s0, s1) * (1.0 / SB_DH)
        return x * lax.rsqrt(ms + EPS) * g

    kc = 512

    @pl.when(i == 0)
    def _():
        def body(c, carry):
            r0 = pl.multiple_of(c * kc, kc)
            kk = k_ref[0, pl.ds(r0, kc), :].astype(F32)
            kn_scr[pl.ds(r0, kc), :] = head_rms(kk, kg_ref[...]).astype(BF16)
            return carry
        lax.fori_loop(0, S // kc, body, 0)

    qn = head_rms(q_ref[0].astype(F32), qg_ref[...]) * (SB_DH ** -0.5)
    qa = (jnp.where(half0, qn, 0.0).astype(BF16), jnp.where(half0, 0.0, qn).astype(BF16))
    t_idx = lax.broadcasted_iota(I32, (tq, tq), 0)
    s_idx = lax.broadcasted_iota(I32, (tq, tq), 1)
    past = s_idx < t_idx
    upper = jnp.where(t_idx > s_idx, 1.0, 0.0).astype(BF16)

    def block(j, carry, diag):
        o, r_a, r_b = carry
        r0 = pl.multiple_of(j * tq, tq)
        kj = kn_scr[pl.ds(r0, tq), :]
        vj = v_ref[0, pl.ds(r0, tq), :]
        new_r = []
        for a, r in ((0, r_a), (1, r_b)):
            z = _dot_nt(qa[a], kj)
            sp = _softplus(z)
            if diag:
                sp = jnp.where(past, sp, 0.0)
            between = _dot(sp.astype(BF16), upper)
            w = jnp.exp(z - sp - between - r)
            if diag:
                w = jnp.where(past, w, 0.0)
            pv = _dot(w.astype(BF16), vj)
            o = o + (jnp.where(half0, pv, 0.0) if a == 0 else jnp.where(half0, 0.0, pv))
            new_r.append(r + jnp.sum(sp, axis=-1, keepdims=True))
        return o, new_r[0], new_r[1]

    carry = (jnp.zeros((tq, LANES), F32), jnp.zeros((tq, 1), F32), jnp.zeros((tq, 1), F32))
    carry = block(i, carry, True)
    carry = lax.fori_loop(0, i, lambda n, c: block(i - 1 - n, c, False), carry)
    o_ref[...] = carry[0].astype(BF16)


def _stickbreak(p3, qg, kg, B, S):
    T = B * S
    tq = 256
    nq = S // tq
    npair = BRANCH_W // LANES
    vec = pl.BlockSpec((1, LANES), lambda b, p, i: (0, 0))
    return pl.pallas_call(
        functools.partial(_sb_kernel, S=S, tq=tq),
        grid=(B, npair, nq),
        in_specs=[pl.BlockSpec((1, tq, LANES), lambda b, p, i: (G_SQ + p, b * nq + i, 0)),
                  pl.BlockSpec((1, S, LANES), lambda b, p, i: (G_SK + p, b, 0)),
                  pl.BlockSpec((1, S, LANES), lambda b, p, i: (G_SV + p, b, 0)),
                  vec, vec],
        out_specs=pl.BlockSpec((tq, LANES), lambda b, p, i: (b * nq + i, p)),
        out_shape=jax.ShapeDtypeStruct((T, BRANCH_W), BF16),
        scratch_shapes=[pltpu.VMEM((S, LANES), BF16)],
        compiler_params=_params("parallel", "parallel", "arbitrary"),
        name="stickbreak_attn",
    )(p3, p3, p3, qg, kg)


def _ret_kernel(q_ref, k_ref, v_ref, g_ref, ng_ref, o_ref, st_scr, *, ts):
    C = RT_CHUNK

    @pl.when(pl.program_id(1) == 0)
    def _():
        st_scr[...] = jnp.zeros_like(st_scr)

    rel = (lax.broadcasted_iota(I32, (C, C), 0) - lax.broadcasted_iota(I32, (C, C), 1)).astype(F32)
    pos = lax.broadcasted_iota(I32, (C, LANES), 0).astype(F32)
    ng = ng_ref[...]
    scale = RT_DH ** -0.5
    for h in range(RT_HEADS):
        sl = slice(h * LANES, (h + 1) * LANES)
        lg = math.log1p(-(2.0 ** (-5 - h)))
        dm = jnp.where(rel >= 0, jnp.exp(lg * jnp.maximum(rel, 0.0)), 0.0) * scale
        qd = jnp.exp(lg * (pos + 1.0))
        kd = jnp.exp(lg * (C - 1.0 - pos)) * scale
        cd = math.exp(lg * C)
        for c in range(ts // C):
            rs = slice(c * C, (c + 1) * C)
            q = q_ref[h, rs, :]
            k = k_ref[h, rs, :]
            v = v_ref[h, rs, :]
            intra = _dot((_dot_nt(q, k) * dm).astype(BF16), v)
            st = st_scr[h]
            inter = _dot((q.astype(F32) * qd).astype(BF16), st.astype(BF16))
            st_scr[h] = cd * st + _dot_tn((k.astype(F32) * kd).astype(BF16), v)
            o = intra + inter
            o = o * lax.rsqrt(jnp.mean(o * o, axis=-1, keepdims=True) + EPS) * ng[:, sl]
            gg = g_ref[h, rs, :].astype(F32)
            o_ref[rs, sl] = (o * _silu(gg)).astype(BF16)


def _retention(p3, ng, B, S):
    T = B * S
    ts = 512
    nt = S // ts

    def spec(g0):
        return pl.BlockSpec((RT_HEADS, ts, LANES), lambda b, i: (g0 // RT_HEADS, b * nt + i, 0))

    return pl.pallas_call(
        functools.partial(_ret_kernel, ts=ts),
        grid=(B, nt),
        in_specs=[spec(G_RQ), spec(G_RK), spec(G_RV), spec(G_RG),
                  pl.BlockSpec((1, RT_HEADS * RT_DH), lambda b, i: (0, 0))],
        out_specs=pl.BlockSpec((ts, BRANCH_W), lambda b, i: (b * nt + i, 0)),
        out_shape=jax.ShapeDtypeStruct((T, BRANCH_W), BF16),
        scratch_shapes=[pltpu.VMEM((RT_HEADS, RT_DH, RT_DH), F32)],
        compiler_params=_params("parallel", "arbitrary"),
        name="retention",
    )(p3, p3, p3, p3, ng)


def _merge_kernel(x_ref, o0, o1, o2, o3, ng_ref, sc_ref, sh_ref, ga_ref, wg_ref, bg_ref, wb_ref, wo_ref, o_ref):
    D = D_MODEL
    x = x_ref[...]
    hb = _rms_mod(x, ng_ref[...], sc_ref[0], sh_ref[0]).astype(BF16)
    merged = None
    for n, oref in enumerate((o0, o1, o2, o3)):
        gate = jax.nn.sigmoid(_dot(hb, wg_ref[0, :, n * D:(n + 1) * D]) + bg_ref[:, n * D:(n + 1) * D])
        y = _dot(oref[...], wb_ref[0, n * BRANCH_W:(n + 1) * BRANCH_W, :])
        merged = gate * y if merged is None else merged + gate * y
    out = _dot(merged.astype(BF16), wo_ref[0])
    o_ref[...] = x + ga_ref[0] * out


def _merge(x2, outs, ng, sc, sh, ga, w_gate_b, b_gate, w_branch_b, w_out_b, l, S):
    T, D = x2.shape
    tm = 512
    tpb = S // tm
    mod = pl.BlockSpec((1, 1, D), lambda i: (i // tpb, 0, 0))
    osp = pl.BlockSpec((tm, BRANCH_W), lambda i: (i, 0))
    once = dict(pipeline_mode=pl.Buffered(1))
    return pl.pallas_call(
        _merge_kernel,
        grid=(T // tm,),
        in_specs=[pl.BlockSpec((tm, D), lambda i: (i, 0)), osp, osp, osp, osp,
                  pl.BlockSpec((1, D), lambda i: (0, 0)), mod, mod, mod,
                  pl.BlockSpec((1, D, N_BRANCH * D), lambda i: (l, 0, 0), **once),
                  pl.BlockSpec((1, N_BRANCH * D), lambda i: (0, 0)),
                  pl.BlockSpec((1, N_BRANCH * BRANCH_W, D), lambda i: (l, 0, 0), **once),
                  pl.BlockSpec((1, D, D), lambda i: (l, 0, 0), **once)],
        out_specs=pl.BlockSpec((tm, D), lambda i: (i, 0)),
        out_shape=jax.ShapeDtypeStruct((T, D), F32),
        compiler_params=_params("parallel"),
        name="branch_merge",
    )(x2, *outs, ng, sc, sh, ga, w_gate_b, b_gate, w_branch_b, w_out_b)


def _route_kernel(x_ref, ng_ref, sc_ref, sh_ref, w_ref, b_ref, h_ref, info_ref, cnt_ref, carry_scr, *, tm):
    @pl.when(pl.program_id(0) == 0)
    def _():
        carry_scr[...] = jnp.zeros_like(carry_scr)

    h = _rms_mod(x_ref[...], ng_ref[...], sc_ref[0], sh_ref[0])
    h_ref[...] = h
    w = w_ref[...]
    h_hi, h_mid, _ = _split3(h)
    w_hi, w_mid, _ = _split3(w)
    lg = _dot(h_hi, w_hi) + _dot(h_mid, w_hi) + _dot(h_hi, w_mid) + b_ref[...]

    lane = lax.broadcasted_iota(I32, (tm, LANES), 1)
    lane_f = lane.astype(F32)
    neg = -jnp.inf
    big = float(LANES)

    def first_argmax(vals):
        m = jnp.max(vals, axis=-1, keepdims=True)
        idx = jnp.min(jnp.where(vals == m, lane_f, big), axis=-1, keepdims=True)
        return m, idx

    gmask = lane < N_GROUPS
    gl = jnp.where(gmask, lg, neg)
    gmax, gidx = first_argmax(gl)
    gsum = jnp.sum(jnp.where(gmask, jnp.exp(lg - gmax), 0.0), axis=-1, keepdims=True)
    g_p = 1.0 / gsum
    lo = N_GROUPS + EXP_PER_GROUP * gidx
    emask = (lane_f >= lo) & (lane_f < lo + EXP_PER_GROUP)
    el = jnp.where(emask, lg, neg)
    v1, i1 = first_argmax(el)
    el2 = jnp.where(lane_f == i1, neg, el)
    v2, i2 = first_argmax(el2)
    e2x = jnp.exp(v2 - v1)
    p1 = 1.0 / (1.0 + e2x)
    p2 = e2x * p1
    e1 = i1 - N_GROUPS
    e2 = i2 - N_GROUPS

    onehot = jnp.where((lane_f == e1) | (lane_f == e2), 1.0, 0.0)
    rr = lax.broadcasted_iota(I32, (tm, tm), 0)
    cc = lax.broadcasted_iota(I32, (tm, tm), 1)
    strict = jnp.where(cc < rr, 1.0, 0.0).astype(BF16)
    prefix = _dot(strict, onehot.astype(BF16)) + carry_scr[0:1, :]
    rank1 = jnp.sum(jnp.where(lane_f == e1, prefix, 0.0), axis=-1, keepdims=True)
    rank2 = jnp.sum(jnp.where(lane_f == e2, prefix, 0.0), axis=-1, keepdims=True)
    total = carry_scr[0:1, :] + jnp.sum(onehot, axis=0, keepdims=True)
    carry_scr[...] = jnp.broadcast_to(total, carry_scr.shape)
    cnt_ref[...] = jnp.broadcast_to(total, cnt_ref.shape)

    info = jnp.where(lane == 0, e1, 0.0)
    info = jnp.where(lane == 1, e2, info)
    info = jnp.where(lane == 2, g_p * p1, info)
    info = jnp.where(lane == 3, g_p * p2, info)
    info = jnp.where(lane == 4, rank1, info)
    info = jnp.where(lane == 5, rank2, info)
    info_ref[...] = info


def _route(x2, ng, sc, sh, w_r, b_r, S):
    T, D = x2.shape
    tm = 512
    tpb = S // tm
    mod = pl.BlockSpec((1, 1, D), lambda i: (i // tpb, 0, 0))
    return pl.pallas_call(
        functools.partial(_route_kernel, tm=tm),
        grid=(T // tm,),
        in_specs=[pl.BlockSpec((tm, D), lambda i: (i, 0)),
                  pl.BlockSpec((1, D), lambda i: (0, 0)), mod, mod,
                  pl.BlockSpec((D, LANES), lambda i: (0, 0)),
                  pl.BlockSpec((1, LANES), lambda i: (0, 0))],
        out_specs=[pl.BlockSpec((tm, D), lambda i: (i, 0)),
                   pl.BlockSpec((tm, LANES), lambda i: (i, 0)),
                   pl.BlockSpec((8, LANES), lambda i: (0, 0))],
        out_shape=[jax.ShapeDtypeStruct((T, D), F32),
                   jax.ShapeDtypeStruct((T, LANES), F32),
                   jax.ShapeDtypeStruct((8, LANES), F32)],
        scratch_shapes=[pltpu.VMEM((8, LANES), F32)],
        compiler_params=_params("arbitrary"),
        name="moe_route",
    )(x2, ng, sc, sh, w_r, b_r)


def _row_copy(src_ref, src_row, dst_ref, dst_row, sem):
    return pltpu.make_async_copy(src_ref.at[pl.ds(src_row, 1), :], dst_ref.at[pl.ds(dst_row, 1), :], sem)


def _dispatch_kernel(dest_ref, h_ref, xb_in_ref, xb_ref, sem, *, tm):
    del xb_in_ref

    def start(r, carry):
        for k in range(2):
            _row_copy(h_ref, r, xb_ref, dest_ref[k, r], sem).start()
        return carry

    def wait(r, carry):
        for k in range(2):
            _row_copy(h_ref, r, xb_ref, dest_ref[k, r], sem).wait()
        return carry

    lax.fori_loop(0, tm, start, 0)
    lax.fori_loop(0, tm, wait, 0)


def _dispatch(h2, dest, n_rows):
    T, D = h2.shape
    tm = 256
    return pl.pallas_call(
        functools.partial(_dispatch_kernel, tm=tm),
        grid=(T // tm,),
        in_specs=[pl.BlockSpec((2, tm), lambda i: (0, i), memory_space=pltpu.SMEM),
                  pl.BlockSpec((tm, D), lambda i: (i, 0)),
                  pl.BlockSpec(memory_space=pl.ANY)],
        out_specs=pl.BlockSpec(memory_space=pl.ANY),
        out_shape=jax.ShapeDtypeStruct((n_rows, D), F32),
        scratch_shapes=[pltpu.SemaphoreType.DMA(())],
        input_output_aliases={2: 0},
        compiler_params=_params("arbitrary"),
        name="moe_dispatch",
    )(dest, h2, jnp.zeros((n_rows, D), F32))


def _expert_kernel(be_ref, x_ref, w1_ref, w3_ref, w2_ref, y_ref):
    del be_ref
    x = x_ref[...].astype(BF16)
    a = _silu(_dot(x, w1_ref[0, 0])) * _dot(x, w3_ref[0, 0])
    y_ref[...] = _dot(a.astype(BF16), w2_ref[0, 0])


def _experts(xb, blk_e, w1_b, w3_b, w2_b, l):
    P, D = xb.shape
    R = MOE_ROWS
    grid_spec = pltpu.PrefetchScalarGridSpec(
        num_scalar_prefetch=1,
        grid=(P // R,),
        in_specs=[pl.BlockSpec((R, D), lambda i, be: (i, 0)),
                  pl.BlockSpec((1, 1, D, D_EXPERT), lambda i, be: (l, be[i], 0, 0)),
                  pl.BlockSpec((1, 1, D, D_EXPERT), lambda i, be: (l, be[i], 0, 0)),
                  pl.BlockSpec((1, 1, D_EXPERT, D), lambda i, be: (l, be[i], 0, 0))],
        out_specs=pl.BlockSpec((R, D), lambda i, be: (i, 0)),
    )
    return pl.pallas_call(
        _expert_kernel,
        grid_spec=grid_spec,
        out_shape=jax.ShapeDtypeStruct((P, D), F32),
        compiler_params=_params("arbitrary"),
        name="moe_experts",
    )(blk_e, xb, w1_b, w3_b, w2_b)


def _combine_kernel(dest_ref, x_ref, info_ref, ga_ref, yb_ref, o_ref, buf, sem, *, tm):
    def start(r, carry):
        for k in range(2):
            _row_copy(yb_ref, dest_ref[k, r], buf.at[k], r, sem).start()
        return carry

    def wait(r, carry):
        for k in range(2):
            _row_copy(yb_ref, dest_ref[k, r], buf.at[k], r, sem).wait()
        return carry

    lax.fori_loop(0, tm, start, 0)
    lax.fori_loop(0, tm, wait, 0)
    lane = lax.broadcasted_iota(I32, (tm, LANES), 1)
    info = info_ref[...]
    g_a = jnp.sum(jnp.where(lane == 2, info, 0.0), axis=-1, keepdims=True)
    g_b = jnp.sum(jnp.where(lane == 3, info, 0.0), axis=-1, keepdims=True)
    o_ref[...] = x_ref[...] + ga_ref[0] * (g_a * buf[0] + g_b * buf[1])


def _combine(x2, info, dest, ga, yb, S):
    T, D = x2.shape
    tm = 256
    tpb = S // tm
    return pl.pallas_call(
        functools.partial(_combine_kernel, tm=tm),
        grid=(T // tm,),
        in_specs=[pl.BlockSpec((2, tm), lambda i: (0, i), memory_space=pltpu.SMEM),
                  pl.BlockSpec((tm, D), lambda i: (i, 0)),
                  pl.BlockSpec((tm, LANES), lambda i: (i, 0)),
                  pl.BlockSpec((1, 1, D), lambda i: (i // tpb, 0, 0)),
                  pl.BlockSpec(memory_space=pl.ANY)],
        out_specs=pl.BlockSpec((tm, D), lambda i: (i, 0)),
        out_shape=jax.ShapeDtypeStruct((T, D), F32),
        scratch_shapes=[pltpu.VMEM((2, tm, D), F32), pltpu.SemaphoreType.DMA(())],
        compiler_params=_params("arbitrary"),
        name="moe_combine",
    )(dest, x2, info, ga, yb)


def _moe(x2, ng, sc, sh, ga, w_r, b_r, w1_b, w3_b, w2_b, l, S):
    T, D = x2.shape
    R = MOE_ROWS
    n_rows = (T * 2 + N_EXPERTS * (R - 1) + R - 1) // R * R
    h2, info, cnt = _route(x2, ng, sc, sh, w_r, b_r, S)
    e = info[:, 0:2].astype(I32)
    rank = info[:, 4:6].astype(I32)
    counts = cnt[0, :N_EXPERTS].astype(I32)
    padded = (counts + R - 1) // R * R
    pad_end = jnp.cumsum(padded)
    pad_start = pad_end - padded
    dest = (pad_start[e] + rank).T
    blk_e = jnp.minimum(jnp.searchsorted(pad_end, jnp.arange(n_rows // R, dtype=I32) * R, side='right'),
                        N_EXPERTS - 1).astype(I32)
    xb = _dispatch(h2, dest, n_rows)
    yb = _experts(xb, blk_e, w1_b, w3_b, w2_b, l)
    return _combine(x2, info, dest, ga, yb, S)


def kernel(x, c, ada_w, ada_b, norm1_g, norm2_g, w_in, hgrn_lb, hgrn_norm_g, conv_w, conv_b, conv_ln_g,
           conv_ln_b, sb_qnorm_g, sb_knorm_g, ret_norm_g, w_branch, w_gate, b_gate, w_out, router_group_w,
           router_group_b, router_expert_w, router_expert_b, expert_w1, expert_w3, expert_w2):
    B, S, D = x.shape
    L = ada_w.shape[0]
    T = B * S
    sm = jax.nn.softmax(hgrn_lb.astype(F32), axis=0)
    lower_bounds = jnp.cumsum(sm, axis=0) - sm[0]
    mods = _mods(c, ada_w, ada_b).reshape(L, B, 6, 1, D)
    w_in_b = w_in.astype(BF16)
    w_gate_b = w_gate.astype(BF16)
    w_branch_b = w_branch.astype(BF16)
    w_out_b = w_out.astype(BF16)
    w1_b = expert_w1.astype(BF16)
    w3_b = expert_w3.astype(BF16)
    w2_b = expert_w2.astype(BF16)
    n_r = N_GROUPS + N_EXPERTS
    w_r = jnp.zeros((L, D, LANES), F32).at[:, :, :N_GROUPS].set(router_group_w).at[:, :, N_GROUPS:n_r].set(router_expert_w)
    b_r = jnp.zeros((L, 1, LANES), F32).at[:, 0, :N_GROUPS].set(router_group_b).at[:, 0, N_GROUPS:n_r].set(router_expert_b)

    x2 = x.reshape(T, D)
    for l in range(L):
        sh1, sc1, g1, sh2, sc2, g2 = (mods[l, :, k] for k in range(6))
        n1 = norm1_g[l][None, :]
        p3 = _in_proj(x2, n1, sc1, sh1, w_in_b, l, S)
        o_hg = _hgrn(p3, lower_bounds[l][None, :], hgrn_norm_g[l][None, :], B, S)
        o_cv = _conv(p3, conv_w[l], conv_b[l][None, :], conv_ln_g[l][None, :], conv_ln_b[l][None, :], B, S)
        qg = jnp.tile(sb_qnorm_g[l], 2)[None, :]
        kg = jnp.tile(sb_knorm_g[l], 2)[None, :]
        o_sb = _stickbreak(p3, qg, kg, B, S)
        o_rt = _retention(p3, ret_norm_g[l][None, :], B, S)
        x2 = _merge(x2, (o_hg, o_cv, o_sb, o_rt), n1, sc1, sh1, g1, w_gate_b, b_gate[l][None, :],
                    w_branch_b, w_out_b, l, S)
        x2 = _moe(x2, norm2_g[l][None, :], sc2, sh2, g2, w_r[l], b_r[l], w1_b, w3_b, w2_b, l, S)
    return x2.reshape(B, S, D)
```

```python
import functools
import math

import jax
import jax.numpy as jnp
from jax import lax
from jax.experimental import pallas as pl
from jax.experimental.pallas import tpu as pltpu

F32 = jnp.float32
BF16 = jnp.bfloat16
I32 = jnp.int32

LANES = 128
SUB = 8
VMEM_LIMIT = 56 * 1024 * 1024

D_MODEL = 1024
EPS = 1e-6
LOG_TINY = -87.0
HG_HEADS, HG_DK, HG_CHUNK = 4, 128, 16
CV_W, CV_K = 512, 31
CV_HALO = 32
SB_DH = 64
RT_HEADS, RT_DH, RT_CHUNK = 4, 128, 128
N_BRANCH = 4
BRANCH_W = 512
N_GROUPS, EXP_PER_GROUP, N_EXPERTS = 4, 8, 32
D_EXPERT = 512
MOE_ROWS = 256
DMA_UNROLL = 8
SB_DONE_AT = 104.0

G_HQ, G_HF, G_HI, G_HG = 0, 4, 8, 12
G_CA, G_CG = 16, 20
G_SQ, G_SK, G_SV = 24, 28, 32
G_RQ, G_RK, G_RV, G_RG = 36, 40, 44, 48
N_COL_GROUPS = 52


def _dot(a, b):
    return jnp.dot(a, b, preferred_element_type=F32)


def _dot_nt(a, b):
    return lax.dot_general(a, b, (((1,), (1,)), ((), ())), preferred_element_type=F32)


def _dot_tn(a, b):
    return lax.dot_general(a, b, (((0,), (0,)), ((), ())), preferred_element_type=F32)


def _split3(x):
    hi = x.astype(BF16)
    r1 = x - hi.astype(F32)
    mid = r1.astype(BF16)
    lo = (r1 - mid.astype(F32)).astype(BF16)
    return hi, mid, lo


def _rms_mod(x, g, sc, sh):
    ms = jnp.mean(x * x, axis=-1, keepdims=True)
    return x * lax.rsqrt(ms + EPS) * g * (1.0 + sc) + sh


def _silu(x):
    return x * jax.nn.sigmoid(x)


def _softplus(z):
    return jnp.maximum(z, 0.0) + jnp.log1p(jnp.exp(-jnp.abs(z)))


def _params(*sem):
    return pltpu.CompilerParams(dimension_semantics=sem, vmem_limit_bytes=VMEM_LIMIT)


def _mods_kernel(c_ref, w_ref, b_ref, o_ref):
    c = c_ref[...]
    sc = _silu(c)
    w = w_ref[0]
    acc = None
    for cp in _split3(sc):
        for wp in _split3(w)[:2]:
            t = _dot(cp, wp)
            acc = t if acc is None else acc + t
    o_ref[0] = acc + b_ref[0]


def _mods(c, ada_w, ada_b):
    L, D, E = ada_w.shape
    B = c.shape[0]
    rows = 8
    tn = 1536
    cp = jnp.zeros((rows, D), F32).at[:B].set(c)
    out = pl.pallas_call(
        _mods_kernel,
        grid=(L, E // tn),
        in_specs=[pl.BlockSpec((rows, D), lambda l, j: (0, 0)),
                  pl.BlockSpec((1, D, tn), lambda l, j: (l, 0, j)),
                  pl.BlockSpec((1, 1, tn), lambda l, j: (l, 0, j))],
        out_specs=pl.BlockSpec((1, rows, tn), lambda l, j: (l, 0, j)),
        out_shape=jax.ShapeDtypeStruct((L, rows, E), F32),
        compiler_params=_params("parallel", "parallel"),
        name="adaln_mods",
    )(cp, ada_w, ada_b.reshape(L, 1, E))
    return out[:, :B]


def _in_kernel(x_ref, g_ref, sc_ref, sh_ref, w_ref, o_ref, *, tn):
    hb = _rms_mod(x_ref[...], g_ref[...], sc_ref[0], sh_ref[0]).astype(BF16)
    gpt = tn // LANES
    for j in range(w_ref.shape[2] // tn):
        res = _dot(hb, w_ref[0, :, j * tn:(j + 1) * tn])
        for q in range(gpt):
            o_ref[j * gpt + q] = res[:, q * LANES:(q + 1) * LANES].astype(BF16)


def _in_proj(x2, g, sc, sh, w_in_b, l, S):
    T, D = x2.shape
    tm, tn = 512, 512
    cols = w_in_b.shape[2]
    tpb = S // tm
    mod = pl.BlockSpec((1, 1, D), lambda i: (i // tpb, 0, 0))
    return pl.pallas_call(
        functools.partial(_in_kernel, tn=tn),
        grid=(T // tm,),
        in_specs=[pl.BlockSpec((tm, D), lambda i: (i, 0)),
                  pl.BlockSpec((1, D), lambda i: (0, 0)), mod, mod,
                  pl.BlockSpec((1, D, cols), lambda i: (l, 0, 0), pipeline_mode=pl.Buffered(1))],
        out_specs=pl.BlockSpec((cols // LANES, tm, LANES), lambda i: (0, i, 0)),
        out_shape=jax.ShapeDtypeStruct((cols // LANES, T, LANES), BF16),
        compiler_params=_params("parallel"),
        name="in_proj",
    )(x2, g, sc, sh, w_in_b)


def _hgrn_kernel(q_ref, f_ref, i_ref, g_ref, lb_ref, ng_ref, o_ref, b_scr, k_scr, q_scr, st_scr, *, ts):
    C = HG_CHUNK

    @pl.when(pl.program_id(1) == 0)
    def _():
        st_scr[...] = jnp.zeros_like(st_scr)

    lb = lb_ref[...]
    log_lb = jnp.maximum(jnp.log(jnp.maximum(lb, 1e-30)), LOG_TINY)
    log_1m = jnp.log1p(-lb)
    one_m = 1.0 - lb
    ng = ng_ref[...]
    rr = lax.broadcasted_iota(I32, (ts, ts), 0)
    cc = lax.broadcasted_iota(I32, (ts, ts), 1)
    shift = C.bit_length() - 1
    tri = jnp.where((jnp.right_shift(rr, shift) == jnp.right_shift(cc, shift)) & (cc <= rr), 1.0, 0.0).astype(BF16)

    for h in range(HG_HEADS):
        sl = slice(h * LANES, (h + 1) * LANES)
        fx = f_ref[h].astype(F32)
        lsig = jnp.minimum(fx, 0.0) - jnp.log1p(jnp.exp(-jnp.abs(fx)))
        a = log_lb[:, sl]
        bb = log_1m[:, sl] + lsig
        log_f = jnp.maximum(a, bb) + jnp.log1p(jnp.exp(-jnp.abs(a - bb)))
        hi, mid, lo = _split3(log_f)
        b_scr[h] = _dot(tri, hi) + _dot(tri, mid) + _dot(tri, lo)
        k_scr[h] = one_m[:, sl] * jax.nn.sigmoid(-fx)
        q_scr[h] = _silu(q_ref[h].astype(F32))

    rows = lax.broadcasted_iota(I32, (SUB, LANES), 0)

    def chunk(c, carry):
        r0 = pl.multiple_of(c * C, C)
        for h in range(HG_HEADS):
            sl = slice(h * LANES, (h + 1) * LANES)
            b = b_scr[h, pl.ds(r0, C), :]
            q = q_scr[h, pl.ds(r0, C), :]
            k = k_scr[h, pl.ds(r0, C), :]
            v = i_ref[h, pl.ds(r0, C), :].astype(F32)
            b_last = b[C - 1:C, :]
            q_dec = q * jnp.exp(b)
            k_st = k * jnp.exp(b_last - b)
            st = st_scr[h]
            inter = _dot_nt(q_dec.astype(BF16), st.astype(BF16))
            st_scr[h] = st * jnp.exp(b_last) + _dot_tn(v.astype(BF16), k_st.astype(BF16))
            acc = [inter[:SUB, :], inter[SUB:, :]]
            for s in range(C):
                b_s, k_s, v_s = b[s:s + 1, :], k[s:s + 1, :], v[s:s + 1, :]
                for half in range(s // SUB, C // SUB):
                    rs = slice(half * SUB, (half + 1) * SUB)
                    diff = b[rs, :] - b_s
                    if s > half * SUB:
                        diff = jnp.where(rows >= s - half * SUB, diff, -1e30)
                    sc = jnp.sum(q[rs, :] * k_s * jnp.exp(diff), axis=-1, keepdims=True)
                    acc[half] = acc[half] + sc * v_s
            acc = jnp.concatenate(acc, axis=0)
            o = acc * lax.rsqrt(jnp.mean(acc * acc, axis=-1, keepdims=True) + EPS) * ng[:, sl]
            gg = g_ref[h, pl.ds(r0, C), :].astype(F32)
            o_ref[pl.ds(r0, C), sl] = (o * _silu(gg)).astype(BF16)
        return carry

    lax.fori_loop(0, ts // C, chunk, 0)


def _hgrn(p3, lb, ng, B, S):
    T = B * S
    ts = 128
    nt = S // ts

    def spec(g0):
        return pl.BlockSpec((HG_HEADS, ts, LANES), lambda b, i: (g0 // HG_HEADS, b * nt + i, 0))

    return pl.pallas_call(
        functools.partial(_hgrn_kernel, ts=ts),
        grid=(B, nt),
        in_specs=[spec(G_HQ), spec(G_HF), spec(G_HI), spec(G_HG),
                  pl.BlockSpec((1, HG_HEADS * HG_DK), lambda b, i: (0, 0)),
                  pl.BlockSpec((1, HG_HEADS * HG_DK), lambda b, i: (0, 0))],
        out_specs=pl.BlockSpec((ts, BRANCH_W), lambda b, i: (b * nt + i, 0)),
        out_shape=jax.ShapeDtypeStruct((T, BRANCH_W), BF16),
        scratch_shapes=[pltpu.VMEM((HG_HEADS, ts, LANES), F32),
                        pltpu.VMEM((HG_HEADS, ts, LANES), F32),
                        pltpu.VMEM((HG_HEADS, ts, LANES), F32),
                        pltpu.VMEM((HG_HEADS, HG_DK, HG_DK), F32)],
        compiler_params=_params("parallel", "arbitrary"),
        name="hgrn2",
    )(p3, p3, p3, p3, lb, ng)


def _conv_kernel(a_ref, g_ref, w_ref, cb_ref, lg_ref, lbias_ref, o_ref, z_scr, y_scr, *, ts):
    H = CV_HALO

    @pl.when(pl.program_id(1) == 0)
    def _():
        z_scr[0:H, :] = jnp.zeros((H, CV_W), F32)

    @pl.when(pl.program_id(1) > 0)
    def _():
        z_scr[0:H, :] = z_scr[ts:ts + H, :]

    for q in range(CV_W // LANES):
        sl = slice(q * LANES, (q + 1) * LANES)
        z_scr[H:H + ts, sl] = a_ref[q].astype(F32) * jax.nn.sigmoid(g_ref[q].astype(F32))

    off = H - (CV_K - 1)
    for q in range(CV_W // LANES):
        sl = slice(q * LANES, (q + 1) * LANES)
        acc = jnp.zeros((ts, LANES), F32)
        for j in range(CV_K):
            acc = acc + w_ref[j:j + 1, sl] * z_scr[off + j:off + j + ts, sl]
        y_scr[:, sl] = acc + cb_ref[:, sl]

    y = y_scr[...]
    mean = jnp.mean(y, axis=-1, keepdims=True)
    yc = y - mean
    var = jnp.mean(yc * yc, axis=-1, keepdims=True)
    yn = yc * lax.rsqrt(var + EPS) * lg_ref[...] + lbias_ref[...]
    o_ref[...] = _silu(yn).astype(BF16)


def _conv(p3, conv_w, conv_b, ln_g, ln_b, B, S):
    T = B * S
    ts = 256
    nt = S // ts
    ng = CV_W // LANES

    def spec(g0):
        return pl.BlockSpec((ng, ts, LANES), lambda b, i: (g0 // ng, b * nt + i, 0))

    vec = pl.BlockSpec((1, CV_W), lambda b, i: (0, 0))
    return pl.pallas_call(
        functools.partial(_conv_kernel, ts=ts),
        grid=(B, nt),
        in_specs=[spec(G_CA), spec(G_CG), pl.BlockSpec((CV_K, CV_W), lambda b, i: (0, 0)), vec, vec, vec],
        out_specs=pl.BlockSpec((ts, CV_W), lambda b, i: (b * nt + i, 0)),
        out_shape=jax.ShapeDtypeStruct((T, CV_W), BF16),
        scratch_shapes=[pltpu.VMEM((ts + CV_HALO, CV_W), F32), pltpu.VMEM((ts, CV_W), F32)],
        compiler_params=_params("parallel", "arbitrary"),
        name="conformer_conv",
    )(p3, p3, conv_w, conv_b, ln_g, ln_b)


def _sb_kernel(q_ref, k_ref, v_ref, qg_ref, kg_ref, o_ref, kn_scr, *, S, tq):
    i = pl.program_id(2)
    lane = lax.broadcasted_iota(I32, (1, LANES), 1)
    half0 = lane < SB_DH

    def head_rms(x, g):
        x2 = x * x
        s0 = jnp.sum(jnp.where(half0, x2, 0.0), axis=-1, keepdims=True)
        s1 = jnp.sum(jnp.where(half0, 0.0, x2), axis=-1, keepdims=True)
        ms = jnp.where(half0, s0, s1) * (1.0 / SB_DH)
        return x * lax.rsqrt(ms + EPS) * g

    kc = 512

    @pl.when(i == 0)
    def _():
        def body(c, carry):
            r0 = pl.multiple_of(c * kc, kc)
            kk = k_ref[0, pl.ds(r0, kc), :].astype(F32)
            kn_scr[pl.ds(r0, kc), :] = head_rms(kk, kg_ref[...]).astype(BF16)
            return carry
        lax.fori_loop(0, S // kc, body, 0)

    qn = head_rms(q_ref[0].astype(F32), qg_ref[...]) * (SB_DH ** -0.5)
    qa = (jnp.where(half0, qn, 0.0).astype(BF16), jnp.where(half0, 0.0, qn).astype(BF16))
    t_idx = lax.broadcasted_iota(I32, (tq, tq), 0)
    s_idx = lax.broadcasted_iota(I32, (tq, tq), 1)
    past = s_idx < t_idx
    upper = jnp.where(t_idx > s_idx, 1.0, 0.0).astype(BF16)

    def block(j, carry, diag):
        o, r_a, r_b = carry
        r0 = pl.multiple_of(j * tq, tq)
        kj = kn_scr[pl.ds(r0, tq), :]
        vj = v_ref[0, pl.ds(r0, tq), :]
        new_r = []
        for a, r in ((0, r_a), (1, r_b)):
            z = _dot_nt(qa[a], kj)
            sp = _softplus(z)
            if diag:
                sp = jnp.where(past, sp, 0.0)
            between = _dot(sp.astype(BF16), upper)
            w = jnp.exp(z - sp - between - r)
            if diag:
                w = jnp.where(past, w, 0.0)
            pv = _dot(w.astype(BF16), vj)
            o = o + (jnp.where(half0, pv, 0.0) if a == 0 else jnp.where(half0, 0.0, pv))
            new_r.append(r + jnp.sum(sp, axis=-1, keepdims=True))
        return o, new_r[0], new_r[1]

    carry = (jnp.zeros((tq, LANES), F32), jnp.zeros((tq, 1), F32), jnp.zeros((tq, 1), F32))
    carry = block(i, carry, True)

    def more(c):
        j, _, r_a, r_b = c
        return jnp.logical_and(j >= 0, jnp.minimum(jnp.min(r_a), jnp.min(r_b)) < SB_DONE_AT)

    def step(c):
        j, o, r_a, r_b = c
        return (j - 1,) + block(j, (o, r_a, r_b), False)

    out = lax.while_loop(more, step, (i - 1,) + carry)
    o_ref[...] = out[1].astype(BF16)


def _stickbreak(p3, qg, kg, B, S):
    T = B * S
    tq = 256
    nq = S // tq
    npair = BRANCH_W // LANES
    vec = pl.BlockSpec((1, LANES), lambda b, p, i: (0, 0))
    return pl.pallas_call(
        functools.partial(_sb_kernel, S=S, tq=tq),
        grid=(B, npair, nq),
        in_specs=[pl.BlockSpec((1, tq, LANES), lambda b, p, i: (G_SQ + p, b * nq + i, 0)),
                  pl.BlockSpec((1, S, LANES), lambda b, p, i: (G_SK + p, b, 0)),
                  pl.BlockSpec((1, S, LANES), lambda b, p, i: (G_SV + p, b, 0)),
                  vec, vec],
        out_specs=pl.BlockSpec((tq, LANES), lambda b, p, i: (b * nq + i, p)),
        out_shape=jax.ShapeDtypeStruct((T, BRANCH_W), BF16),
        scratch_shapes=[pltpu.VMEM((S, LANES), BF16)],
        compiler_params=_params("parallel", "parallel", "arbitrary"),
        name="stickbreak_attn",
    )(p3, p3, p3, qg, kg)


def _ret_kernel(q_ref, k_ref, v_ref, g_ref, ng_ref, o_ref, st_scr, *, ts):
    C = RT_CHUNK

    @pl.when(pl.program_id(1) == 0)
    def _():
        st_scr[...] = jnp.zeros_like(st_scr)

    rel = (lax.broadcasted_iota(I32, (C, C), 0) - lax.broadcasted_iota(I32, (C, C), 1)).astype(F32)
    pos = lax.broadcasted_iota(I32, (C, LANES), 0).astype(F32)
    ng = ng_ref[...]
    scale = RT_DH ** -0.5
    for h in range(RT_HEADS):
        sl = slice(h * LANES, (h + 1) * LANES)
        lg = math.log1p(-(2.0 ** (-5 - h)))
        dm = jnp.where(rel >= 0, jnp.exp(lg * jnp.maximum(rel, 0.0)), 0.0) * scale
        qd = jnp.exp(lg * (pos + 1.0))
        kd = jnp.exp(lg * (C - 1.0 - pos)) * scale
        cd = math.exp(lg * C)
        for c in range(ts // C):
            rs = slice(c * C, (c + 1) * C)
            q = q_ref[h, rs, :]
            k = k_ref[h, rs, :]
            v = v_ref[h, rs, :]
            intra = _dot((_dot_nt(q, k) * dm).astype(BF16), v)
            st = st_scr[h]
            inter = _dot((q.astype(F32) * qd).astype(BF16), st.astype(BF16))
            st_scr[h] = cd * st + _dot_tn((k.astype(F32) * kd).astype(BF16), v)
            o = intra + inter
            o = o * lax.rsqrt(jnp.mean(o * o, axis=-1, keepdims=True) + EPS) * ng[:, sl]
            gg = g_ref[h, rs, :].astype(F32)
            o_ref[rs, sl] = (o * _silu(gg)).astype(BF16)


def _retention(p3, ng, B, S):
    T = B * S
    ts = 512
    nt = S // ts

    def spec(g0):
        return pl.BlockSpec((RT_HEADS, ts, LANES), lambda b, i: (g0 // RT_HEADS, b * nt + i, 0))

    return pl.pallas_call(
        functools.partial(_ret_kernel, ts=ts),
        grid=(B, nt),
        in_specs=[spec(G_RQ), spec(G_RK), spec(G_RV), spec(G_RG),
                  pl.BlockSpec((1, RT_HEADS * RT_DH), lambda b, i: (0, 0))],
        out_specs=pl.BlockSpec((ts, BRANCH_W), lambda b, i: (b * nt + i, 0)),
        out_shape=jax.ShapeDtypeStruct((T, BRANCH_W), BF16),
        scratch_shapes=[pltpu.VMEM((RT_HEADS, RT_DH, RT_DH), F32)],
        compiler_params=_params("parallel", "arbitrary"),
        name="retention",
    )(p3, p3, p3, p3, ng)


def _merge_kernel(x_ref, o0, o1, o2, o3, ng_ref, sc_ref, sh_ref, ga_ref, wg_ref, bg_ref, wb_ref, wo_ref, o_ref):
    D = D_MODEL
    x = x_ref[...]
    hb = _rms_mod(x, ng_ref[...], sc_ref[0], sh_ref[0]).astype(BF16)
    merged = None
    for n, oref in enumerate((o0, o1, o2, o3)):
        gate = jax.nn.sigmoid(_dot(hb, wg_ref[0, :, n * D:(n + 1) * D]) + bg_ref[:, n * D:(n + 1) * D])
        y = _dot(oref[...], wb_ref[0, n * BRANCH_W:(n + 1) * BRANCH_W, :])
        merged = gate * y if merged is None else merged + gate * y
    out = _dot(merged.astype(BF16), wo_ref[0])
    o_ref[...] = x + ga_ref[0] * out


def _merge(x2, outs, ng, sc, sh, ga, w_gate_b, b_gate, w_branch_b, w_out_b, l, S):
    T, D = x2.shape
    tm = 512
    tpb = S // tm
    mod = pl.BlockSpec((1, 1, D), lambda i: (i // tpb, 0, 0))
    osp = pl.BlockSpec((tm, BRANCH_W), lambda i: (i, 0))
    once = dict(pipeline_mode=pl.Buffered(1))
    return pl.pallas_call(
        _merge_kernel,
        grid=(T // tm,),
        in_specs=[pl.BlockSpec((tm, D), lambda i: (i, 0)), osp, osp, osp, osp,
                  pl.BlockSpec((1, D), lambda i: (0, 0)), mod, mod, mod,
                  pl.BlockSpec((1, D, N_BRANCH * D), lambda i: (l, 0, 0), **once),
                  pl.BlockSpec((1, N_BRANCH * D), lambda i: (0, 0)),
                  pl.BlockSpec((1, N_BRANCH * BRANCH_W, D), lambda i: (l, 0, 0), **once),
                  pl.BlockSpec((1, D, D), lambda i: (l, 0, 0), **once)],
        out_specs=pl.BlockSpec((tm, D), lambda i: (i, 0)),
        out_shape=jax.ShapeDtypeStruct((T, D), F32),
        compiler_params=_params("parallel"),
        name="branch_merge",
    )(x2, *outs, ng, sc, sh, ga, w_gate_b, b_gate, w_branch_b, w_out_b)


def _route_kernel(x_ref, ng_ref, sc_ref, sh_ref, w_ref, b_ref, h_ref, info_ref, cnt_ref, carry_scr, *, tm):
    @pl.when(pl.program_id(0) == 0)
    def _():
        carry_scr[...] = jnp.zeros_like(carry_scr)

    h = _rms_mod(x_ref[...], ng_ref[...], sc_ref[0], sh_ref[0])
    h_ref[...] = h
    w = w_ref[...]
    h_hi, h_mid, _ = _split3(h)
    w_hi, w_mid, _ = _split3(w)
    lg = _dot(h_hi, w_hi) + _dot(h_mid, w_hi) + _dot(h_hi, w_mid) + b_ref[...]

    lane = lax.broadcasted_iota(I32, (tm, LANES), 1)
    lane_f = lane.astype(F32)
    neg = -jnp.inf
    big = float(LANES)

    def first_argmax(vals):
        m = jnp.max(vals, axis=-1, keepdims=True)
        idx = jnp.min(jnp.where(vals == m, lane_f, big), axis=-1, keepdims=True)
        return m, idx

    gmask = lane < N_GROUPS
    gl = jnp.where(gmask, lg, neg)
    gmax, gidx = first_argmax(gl)
    gsum = jnp.sum(jnp.where(gmask, jnp.exp(lg - gmax), 0.0), axis=-1, keepdims=True)
    g_p = 1.0 / gsum
    lo = N_GROUPS + EXP_PER_GROUP * gidx
    emask = (lane_f >= lo) & (lane_f < lo + EXP_PER_GROUP)
    el = jnp.where(emask, lg, neg)
    v1, i1 = first_argmax(el)
    el2 = jnp.where(lane_f == i1, neg, el)
    v2, i2 = first_argmax(el2)
    e2x = jnp.exp(v2 - v1)
    p1 = 1.0 / (1.0 + e2x)
    p2 = e2x * p1
    e1 = i1 - N_GROUPS
    e2 = i2 - N_GROUPS

    onehot = jnp.where((lane_f == e1) | (lane_f == e2), 1.0, 0.0)
    rr = lax.broadcasted_iota(I32, (tm, tm), 0)
    cc = lax.broadcasted_iota(I32, (tm, tm), 1)
    strict = jnp.where(cc < rr, 1.0, 0.0).astype(BF16)
    prefix = _dot(strict, onehot.astype(BF16)) + carry_scr[0:1, :]
    rank1 = jnp.sum(jnp.where(lane_f == e1, prefix, 0.0), axis=-1, keepdims=True)
    rank2 = jnp.sum(jnp.where(lane_f == e2, prefix, 0.0), axis=-1, keepdims=True)
    total = carry_scr[0:1, :] + jnp.sum(onehot, axis=0, keepdims=True)
    carry_scr[...] = jnp.broadcast_to(total, carry_scr.shape)
    cnt_ref[...] = jnp.broadcast_to(total, cnt_ref.shape)

    info = jnp.where(lane == 0, e1, 0.0)
    info = jnp.where(lane == 1, e2, info)
    info = jnp.where(lane == 2, g_p * p1, info)
    info = jnp.where(lane == 3, g_p * p2, info)
    info = jnp.where(lane == 4, rank1, info)
    info = jnp.where(lane == 5, rank2, info)
    info_ref[...] = info


def _route(x2, ng, sc, sh, w_r, b_r, S):
    T, D = x2.shape
    tm = 512
    tpb = S // tm
    mod = pl.BlockSpec((1, 1, D), lambda i: (i // tpb, 0, 0))
    return pl.pallas_call(
        functools.partial(_route_kernel, tm=tm),
        grid=(T // tm,),
        in_specs=[pl.BlockSpec((tm, D), lambda i: (i, 0)),
                  pl.BlockSpec((1, D), lambda i: (0, 0)), mod, mod,
                  pl.BlockSpec((D, LANES), lambda i: (0, 0)),
                  pl.BlockSpec((1, LANES), lambda i: (0, 0))],
        out_specs=[pl.BlockSpec((tm, D), lambda i: (i, 0)),
                   pl.BlockSpec((tm, LANES), lambda i: (i, 0)),
                   pl.BlockSpec((8, LANES), lambda i: (0, 0))],
        out_shape=[jax.ShapeDtypeStruct((T, D), F32),
                   jax.ShapeDtypeStruct((T, LANES), F32),
                   jax.ShapeDtypeStruct((8, LANES), F32)],
        scratch_shapes=[pltpu.VMEM((8, LANES), F32)],
        compiler_params=_params("arbitrary"),
        name="moe_route",
    )(x2, ng, sc, sh, w_r, b_r)


def _row_copy(src_ref, src_row, dst_ref, dst_row, sem):
    return pltpu.make_async_copy(src_ref.at[pl.ds(src_row, 1), :], dst_ref.at[pl.ds(dst_row, 1), :], sem)


def _dispatch_kernel(dest_ref, h_ref, xb_in_ref, xb_ref, sem, *, tm):
    del xb_in_ref

    def start(r, carry):
        for k in range(2):
            _row_copy(h_ref, r, xb_ref, dest_ref[k, r], sem).start()
        return carry

    def wait(r, carry):
        for k in range(2):
            _row_copy(h_ref, r, xb_ref, dest_ref[k, r], sem).wait()
        return carry

    lax.fori_loop(0, tm, start, 0, unroll=DMA_UNROLL)
    lax.fori_loop(0, tm, wait, 0, unroll=DMA_UNROLL)


def _dispatch(h2, dest, n_rows):
    T, D = h2.shape
    tm = 256
    return pl.pallas_call(
        functools.partial(_dispatch_kernel, tm=tm),
        grid=(T // tm,),
        in_specs=[pl.BlockSpec((2, tm), lambda i: (0, i), memory_space=pltpu.SMEM),
                  pl.BlockSpec((tm, D), lambda i: (i, 0)),
                  pl.BlockSpec(memory_space=pl.ANY)],
        out_specs=pl.BlockSpec(memory_space=pl.ANY),
        out_shape=jax.ShapeDtypeStruct((n_rows, D), F32),
        scratch_shapes=[pltpu.SemaphoreType.DMA(())],
        input_output_aliases={2: 0},
        compiler_params=_params("arbitrary"),
        name="moe_dispatch",
    )(dest, h2, jnp.zeros((n_rows, D), F32))


def _expert_kernel(be_ref, x_ref, w1_ref, w3_ref, w2_ref, y_ref):
    del be_ref
    x = x_ref[...].astype(BF16)
    a = _silu(_dot(x, w1_ref[0, 0])) * _dot(x, w3_ref[0, 0])
    y_ref[...] = _dot(a.astype(BF16), w2_ref[0, 0])


def _experts(xb, blk_e, w1_b, w3_b, w2_b, l):
    P, D = xb.shape
    R = MOE_ROWS
    grid_spec = pltpu.PrefetchScalarGridSpec(
        num_scalar_prefetch=1,
        grid=(P // R,),
        in_specs=[pl.BlockSpec((R, D), lambda i, be: (i, 0)),
                  pl.BlockSpec((1, 1, D, D_EXPERT), lambda i, be: (l, be[i], 0, 0)),
                  pl.BlockSpec((1, 1, D, D_EXPERT), lambda i, be: (l, be[i], 0, 0)),
                  pl.BlockSpec((1, 1, D_EXPERT, D), lambda i, be: (l, be[i], 0, 0))],
        out_specs=pl.BlockSpec((R, D), lambda i, be: (i, 0)),
    )
    return pl.pallas_call(
        _expert_kernel,
        grid_spec=grid_spec,
        out_shape=jax.ShapeDtypeStruct((P, D), F32),
        compiler_params=_params("arbitrary"),
        name="moe_experts",
    )(blk_e, xb, w1_b, w3_b, w2_b)


def _combine_kernel(dest_ref, x_ref, info_ref, ga_ref, yb_ref, o_ref, buf, sem, *, tm):
    def start(r, carry):
        for k in range(2):
            _row_copy(yb_ref, dest_ref[k, r], buf.at[k], r, sem).start()
        return carry

    def wait(r, carry):
        for k in range(2):
            _row_copy(yb_ref, dest_ref[k, r], buf.at[k], r, sem).wait()
        return carry

    lax.fori_loop(0, tm, start, 0, unroll=DMA_UNROLL)
    lax.fori_loop(0, tm, wait, 0, unroll=DMA_UNROLL)
    lane = lax.broadcasted_iota(I32, (tm, LANES), 1)
    info = info_ref[...]
    g_a = jnp.sum(jnp.where(lane == 2, info, 0.0), axis=-1, keepdims=True)
    g_b = jnp.sum(jnp.where(lane == 3, info, 0.0), axis=-1, keepdims=True)
    o_ref[...] = x_ref[...] + ga_ref[0] * (g_a * buf[0] + g_b * buf[1])


def _combine(x2, info, dest, ga, yb, S):
    T, D = x2.shape
    tm = 256
    tpb = S // tm
    return pl.pallas_call(
        functools.partial(_combine_kernel, tm=tm),
        grid=(T // tm,),
        in_specs=[pl.BlockSpec((2, tm), lambda i: (0, i), memory_space=pltpu.SMEM),
                  pl.BlockSpec((tm, D), lambda i: (i, 0)),
                  pl.BlockSpec((tm, LANES), lambda i: (i, 0)),
                  pl.BlockSpec((1, 1, D), lambda i: (i // tpb, 0, 0)),
                  pl.BlockSpec(memory_space=pl.ANY)],
        out_specs=pl.BlockSpec((tm, D), lambda i: (i, 0)),
        out_shape=jax.ShapeDtypeStruct((T, D), F32),
        scratch_shapes=[pltpu.VMEM((2, tm, D), F32), pltpu.SemaphoreType.DMA(())],
        compiler_params=_params("arbitrary"),
        name="moe_combine",
    )(dest, x2, info, ga, yb)


def _moe(x2, ng, sc, sh, ga, w_r, b_r, w1_b, w3_b, w2_b, l, S):
    T, D = x2.shape
    R = MOE_ROWS
    n_rows = (T * 2 + N_EXPERTS * (R - 1) + R - 1) // R * R
    h2, info, cnt = _route(x2, ng, sc, sh, w_r, b_r, S)
    e = info[:, 0:2].astype(I32)
    rank = info[:, 4:6].astype(I32)
    counts = cnt[0, :N_EXPERTS].astype(I32)
    padded = (counts + R - 1) // R * R
    pad_end = jnp.cumsum(padded)
    pad_start = pad_end - padded
    dest = (pad_start[e] + rank).T
    blk_start = jnp.arange(n_rows // R, dtype=I32) * R
    blk_e = jnp.minimum(jnp.sum((pad_end[None, :] <= blk_start[:, None]).astype(I32), axis=1), N_EXPERTS - 1)
    xb = _dispatch(h2, dest, n_rows)
    yb = _experts(xb, blk_e, w1_b, w3_b, w2_b, l)
    return _combine(x2, info, dest, ga, yb, S)


def kernel(x, c, ada_w, ada_b, norm1_g, norm2_g, w_in, hgrn_lb, hgrn_norm_g, conv_w, conv_b, conv_ln_g,
           conv_ln_b, sb_qnorm_g, sb_knorm_g, ret_norm_g, w_branch, w_gate, b_gate, w_out, router_group_w,
           router_group_b, router_expert_w, router_expert_b, expert_w1, expert_w3, expert_w2):
    B, S, D = x.shape
    L = ada_w.shape[0]
    T = B * S
    sm = jax.nn.softmax(hgrn_lb.astype(F32), axis=0)
    lower_bounds = jnp.cumsum(sm, axis=0) - sm[0]
    mods = _mods(c, ada_w, ada_b).reshape(L, B, 6, 1, D)
    w_in_b = w_in.astype(BF16)
    w_gate_b = w_gate.astype(BF16)
    w_branch_b = w_branch.astype(BF16)
    w_out_b = w_out.astype(BF16)
    w1_b = expert_w1.astype(BF16)
    w3_b = expert_w3.astype(BF16)
    w2_b = expert_w2.astype(BF16)
    n_r = N_GROUPS + N_EXPERTS
    w_r = jnp.zeros((L, D, LANES), F32).at[:, :, :N_GROUPS].set(router_group_w).at[:, :, N_GROUPS:n_r].set(router_expert_w)
    b_r = jnp.zeros((L, 1, LANES), F32).at[:, 0, :N_GROUPS].set(router_group_b).at[:, 0, N_GROUPS:n_r].set(router_expert_b)

    x2 = x.reshape(T, D)
    for l in range(L):
        sh1, sc1, g1, sh2, sc2, g2 = (mods[l, :, k] for k in range(6))
        n1 = norm1_g[l][None, :]
        p3 = _in_proj(x2, n1, sc1, sh1, w_in_b, l, S)
        o_hg = _hgrn(p3, lower_bounds[l][None, :], hgrn_norm_g[l][None, :], B, S)
        o_cv = _conv(p3, conv_w[l], conv_b[l][None, :], conv_ln_g[l][None, :], conv_ln_b[l][None, :], B, S)
        qg = jnp.tile(sb_qnorm_g[l], 2)[None, :]
        kg = jnp.tile(sb_knorm_g[l], 2)[None, :]
        o_sb = _stickbreak(p3, qg, kg, B, S)
        o_rt = _retention(p3, ret_norm_g[l][None, :], B, S)
        x2 = _merge(x2, (o_hg, o_cv, o_sb, o_rt), n1, sc1, sh1, g1, w_gate_b, b_gate[l][None, :],
                    w_branch_b, w_out_b, l, S)
        x2 = _moe(x2, norm2_g[l][None, :], sc2, sh2, g2, w_r[l], b_r[l], w1_b, w3_b, w2_b, l, S)
    return x2.reshape(B, S, D)
```

```python
import functools
import math

import jax
import jax.numpy as jnp
from jax import lax
from jax.experimental import pallas as pl
from jax.experimental.pallas import tpu as pltpu

F32 = jnp.float32
BF16 = jnp.bfloat16
I32 = jnp.int32

LANES = 128
SUB = 8
VMEM_LIMIT = 56 * 1024 * 1024

D_MODEL = 1024
EPS = 1e-6
LOG_TINY = -87.0
LOG2_E = 1.4426950408889634
HG_HEADS, HG_DK, HG_CHUNK = 4, 128, 16
CV_W, CV_K = 512, 31
CV_HALO = 32
SB_DH = 64
RT_HEADS, RT_DH, RT_CHUNK = 4, 128, 128
N_BRANCH = 4
BRANCH_W = 512
N_GROUPS, EXP_PER_GROUP, N_EXPERTS = 4, 8, 32
D_EXPERT = 512
MOE_ROWS = 256
MOE_TILE = 256
DMA_ROWS = 8
SB_DONE_AT = 104.0

G_HQ, G_HF, G_HI, G_HG = 0, 4, 8, 12
G_CA, G_CG = 16, 20
G_SQ, G_SK, G_SV = 24, 28, 32
G_RQ, G_RK, G_RV, G_RG = 36, 40, 44, 48
N_COL_GROUPS = 52


def _dot(a, b):
    return jnp.dot(a, b, preferred_element_type=F32)


def _dot_nt(a, b):
    return lax.dot_general(a, b, (((1,), (1,)), ((), ())), preferred_element_type=F32)


def _dot_tn(a, b):
    return lax.dot_general(a, b, (((0,), (0,)), ((), ())), preferred_element_type=F32)


def _split3(x):
    hi = x.astype(BF16)
    r1 = x - hi.astype(F32)
    mid = r1.astype(BF16)
    lo = (r1 - mid.astype(F32)).astype(BF16)
    return hi, mid, lo


def _rms_mod(x, g, sc, sh):
    ms = jnp.mean(x * x, axis=-1, keepdims=True)
    return x * lax.rsqrt(ms + EPS) * g * (1.0 + sc) + sh


def _silu(x):
    return x * jax.nn.sigmoid(x)


def _params(*sem):
    return pltpu.CompilerParams(dimension_semantics=sem, vmem_limit_bytes=VMEM_LIMIT)


def _mods_kernel(c_ref, w_ref, b_ref, o_ref):
    c = c_ref[...]
    sc = _silu(c)
    w = w_ref[0]
    acc = None
    for cp in _split3(sc):
        for wp in _split3(w)[:2]:
            t = _dot(cp, wp)
            acc = t if acc is None else acc + t
    o_ref[0] = acc + b_ref[0]


def _mods(c, ada_w, ada_b):
    L, D, E = ada_w.shape
    B = c.shape[0]
    rows = 8
    tn = 1536
    cp = jnp.zeros((rows, D), F32).at[:B].set(c)
    out = pl.pallas_call(
        _mods_kernel,
        grid=(L, E // tn),
        in_specs=[pl.BlockSpec((rows, D), lambda l, j: (0, 0)),
                  pl.BlockSpec((1, D, tn), lambda l, j: (l, 0, j)),
                  pl.BlockSpec((1, 1, tn), lambda l, j: (l, 0, j))],
        out_specs=pl.BlockSpec((1, rows, tn), lambda l, j: (l, 0, j)),
        out_shape=jax.ShapeDtypeStruct((L, rows, E), F32),
        compiler_params=_params("parallel", "parallel"),
        name="adaln_mods",
    )(cp, ada_w, ada_b.reshape(L, 1, E))
    return out[:, :B]


def _in_kernel(x_ref, g_ref, sc_ref, sh_ref, w_ref, o_ref, *, tn):
    hb = _rms_mod(x_ref[...], g_ref[...], sc_ref[0], sh_ref[0]).astype(BF16)
    gpt = tn // LANES
    for j in range(w_ref.shape[2] // tn):
        res = _dot(hb, w_ref[0, :, j * tn:(j + 1) * tn])
        for q in range(gpt):
            o_ref[j * gpt + q] = res[:, q * LANES:(q + 1) * LANES].astype(BF16)


def _in_proj(x2, g, sc, sh, w_in_b, l, S):
    T, D = x2.shape
    tm, tn = 512, 512
    cols = w_in_b.shape[2]
    tpb = S // tm
    mod = pl.BlockSpec((1, 1, D), lambda i: (i // tpb, 0, 0))
    return pl.pallas_call(
        functools.partial(_in_kernel, tn=tn),
        grid=(T // tm,),
        in_specs=[pl.BlockSpec((tm, D), lambda i: (i, 0)),
                  pl.BlockSpec((1, D), lambda i: (0, 0)), mod, mod,
                  pl.BlockSpec((1, D, cols), lambda i: (l, 0, 0), pipeline_mode=pl.Buffered(1))],
        out_specs=pl.BlockSpec((cols // LANES, tm, LANES), lambda i: (0, i, 0)),
        out_shape=jax.ShapeDtypeStruct((cols // LANES, T, LANES), BF16),
        compiler_params=_params("parallel"),
        name="in_proj",
    )(x2, g, sc, sh, w_in_b)


def _hgrn_kernel(q_ref, f_ref, i_ref, g_ref, lb_ref, ng_ref, o_ref, b_scr, c_scr, q_scr, st_scr, *, ts):
    C = HG_CHUNK

    @pl.when(pl.program_id(1) == 0)
    def _():
        st_scr[...] = jnp.zeros_like(st_scr)

    lb = lb_ref[...]
    log_lb = jnp.maximum(jnp.log(jnp.maximum(lb, 1e-30)), LOG_TINY)
    log_1m = jnp.log1p(-lb)
    ng = ng_ref[...]
    rr = lax.broadcasted_iota(I32, (ts, ts), 0)
    cc = lax.broadcasted_iota(I32, (ts, ts), 1)
    shift = C.bit_length() - 1
    tri = jnp.where((jnp.right_shift(rr, shift) == jnp.right_shift(cc, shift)) & (cc <= rr), 1.0, 0.0).astype(BF16)

    for h in range(HG_HEADS):
        sl = slice(h * LANES, (h + 1) * LANES)
        fx = f_ref[h].astype(F32)
        log_gate = log_1m[:, sl] + jnp.minimum(fx, 0.0) - jnp.log(1.0 + jnp.exp(-jnp.abs(fx)))
        a = log_lb[:, sl]
        log_f = jnp.maximum(a, log_gate) + jnp.log(1.0 + jnp.exp(-jnp.abs(a - log_gate)))
        hi, mid, lo = _split3(log_f)
        b2 = (_dot(tri, hi) + _dot(tri, mid) + _dot(tri, lo)) * LOG2_E
        b_scr[h] = b2
        c_scr[h] = b2 - (log_gate - fx) * LOG2_E
        q_scr[h] = _silu(q_ref[h].astype(F32))

    rows = lax.broadcasted_iota(I32, (SUB, LANES), 0)

    def chunk(ci, carry):
        r0 = pl.multiple_of(ci * C, C)
        for h in range(HG_HEADS):
            sl = slice(h * LANES, (h + 1) * LANES)
            b = b_scr[h, pl.ds(r0, C), :]
            c = c_scr[h, pl.ds(r0, C), :]
            q = q_scr[h, pl.ds(r0, C), :]
            v = i_ref[h, pl.ds(r0, C), :].astype(F32)
            b_last = b[C - 1:C, :]
            q_dec = q * jnp.exp2(b)
            k_st = jnp.exp2(b_last - c)
            st = st_scr[h]
            inter = _dot_nt(q_dec.astype(BF16), st.astype(BF16))
            st_scr[h] = st * jnp.exp2(b_last) + _dot_tn(v.astype(BF16), k_st.astype(BF16))
            acc = [inter[:SUB, :], inter[SUB:, :]]
            for s in range(C):
                c_s, v_s = c[s:s + 1, :], v[s:s + 1, :]
                for half in range(s // SUB, C // SUB):
                    rs = slice(half * SUB, (half + 1) * SUB)
                    diff = b[rs, :] - c_s
                    if s > half * SUB:
                        diff = jnp.where(rows >= s - half * SUB, diff, -1e30)
                    sc = jnp.sum(q[rs, :] * jnp.exp2(diff), axis=-1, keepdims=True)
                    acc[half] = acc[half] + sc * v_s
            acc = jnp.concatenate(acc, axis=0)
            o = acc * lax.rsqrt(jnp.mean(acc * acc, axis=-1, keepdims=True) + EPS) * ng[:, sl]
            gg = g_ref[h, pl.ds(r0, C), :].astype(F32)
            o_ref[pl.ds(r0, C), sl] = (o * _silu(gg)).astype(BF16)
        return carry

    lax.fori_loop(0, ts // C, chunk, 0)


def _hgrn(p3, lb, ng, B, S):
    T = B * S
    ts = 128
    nt = S // ts

    def spec(g0):
        return pl.BlockSpec((HG_HEADS, ts, LANES), lambda b, i: (g0 // HG_HEADS, b * nt + i, 0))

    return pl.pallas_call(
        functools.partial(_hgrn_kernel, ts=ts),
        grid=(B, nt),
        in_specs=[spec(G_HQ), spec(G_HF), spec(G_HI), spec(G_HG),
                  pl.BlockSpec((1, HG_HEADS * HG_DK), lambda b, i: (0, 0)),
                  pl.BlockSpec((1, HG_HEADS * HG_DK), lambda b, i: (0, 0))],
        out_specs=pl.BlockSpec((ts, BRANCH_W), lambda b, i: (b * nt + i, 0)),
        out_shape=jax.ShapeDtypeStruct((T, BRANCH_W), BF16),
        scratch_shapes=[pltpu.VMEM((HG_HEADS, ts, LANES), F32),
                        pltpu.VMEM((HG_HEADS, ts, LANES), F32),
                        pltpu.VMEM((HG_HEADS, ts, LANES), F32),
                        pltpu.VMEM((HG_HEADS, HG_DK, HG_DK), F32)],
        compiler_params=_params("parallel", "arbitrary"),
        name="hgrn2",
    )(p3, p3, p3, p3, lb, ng)


def _conv_kernel(a_ref, g_ref, w_ref, cb_ref, lg_ref, lbias_ref, o_ref, z_scr, y_scr, *, ts):
    H = CV_HALO

    @pl.when(pl.program_id(1) == 0)
    def _():
        z_scr[0:H, :] = jnp.zeros((H, CV_W), F32)

    @pl.when(pl.program_id(1) > 0)
    def _():
        z_scr[0:H, :] = z_scr[ts:ts + H, :]

    for q in range(CV_W // LANES):
        sl = slice(q * LANES, (q + 1) * LANES)
        z_scr[H:H + ts, sl] = a_ref[q].astype(F32) * jax.nn.sigmoid(g_ref[q].astype(F32))

    off = H - (CV_K - 1)
    for q in range(CV_W // LANES):
        sl = slice(q * LANES, (q + 1) * LANES)
        acc = jnp.zeros((ts, LANES), F32)
        for j in range(CV_K):
            acc = acc + w_ref[j:j + 1, sl] * z_scr[off + j:off + j + ts, sl]
        y_scr[:, sl] = acc + cb_ref[:, sl]

    y = y_scr[...]
    mean = jnp.mean(y, axis=-1, keepdims=True)
    yc = y - mean
    var = jnp.mean(yc * yc, axis=-1, keepdims=True)
    yn = yc * lax.rsqrt(var + EPS) * lg_ref[...] + lbias_ref[...]
    o_ref[...] = _silu(yn).astype(BF16)


def _conv(p3, conv_w, conv_b, ln_g, ln_b, B, S):
    T = B * S
    ts = 256
    nt = S // ts
    ng = CV_W // LANES

    def spec(g0):
        return pl.BlockSpec((ng, ts, LANES), lambda b, i: (g0 // ng, b * nt + i, 0))

    vec = pl.BlockSpec((1, CV_W), lambda b, i: (0, 0))
    return pl.pallas_call(
        functools.partial(_conv_kernel, ts=ts),
        grid=(B, nt),
        in_specs=[spec(G_CA), spec(G_CG), pl.BlockSpec((CV_K, CV_W), lambda b, i: (0, 0)), vec, vec, vec],
        out_specs=pl.BlockSpec((ts, CV_W), lambda b, i: (b * nt + i, 0)),
        out_shape=jax.ShapeDtypeStruct((T, CV_W), BF16),
        scratch_shapes=[pltpu.VMEM((ts + CV_HALO, CV_W), F32), pltpu.VMEM((ts, CV_W), F32)],
        compiler_params=_params("parallel", "arbitrary"),
        name="conformer_conv",
    )(p3, p3, conv_w, conv_b, ln_g, ln_b)


def _sb_kernel(q_ref, k_ref, v_ref, qg_ref, kg_ref, o_ref, kn_scr, *, S, tq, nf):
    i = pl.program_id(2)
    lane = lax.broadcasted_iota(I32, (1, LANES), 1)
    half0 = lane < SB_DH

    def head_rms(x, g):
        x2 = x * x
        s0 = jnp.sum(jnp.where(half0, x2, 0.0), axis=-1, keepdims=True)
        s1 = jnp.sum(jnp.where(half0, 0.0, x2), axis=-1, keepdims=True)
        ms = jnp.where(half0, s0, s1) * (1.0 / SB_DH)
        return x * lax.rsqrt(ms + EPS) * g

    kc = 512

    @pl.when(i == 0)
    def _():
        def body(c, carry):
            r0 = pl.multiple_of(c * kc, kc)
            kk = k_ref[0, pl.ds(r0, kc), :].astype(F32)
            kn_scr[pl.ds(r0, kc), :] = head_rms(kk, kg_ref[...]).astype(BF16)
            return carry
        lax.fori_loop(0, S // kc, body, 0)

    qn = head_rms(q_ref[0].astype(F32), qg_ref[...]) * (SB_DH ** -0.5 * LOG2_E)
    q2 = jnp.concatenate([jnp.where(half0, qn, 0.0), jnp.where(half0, 0.0, qn)], axis=0).astype(BF16)
    t_idx = lax.broadcasted_iota(I32, (tq, tq), 0)
    s_idx = lax.broadcasted_iota(I32, (tq, tq), 1)
    upper = jnp.where(t_idx > s_idx, 1.0, 0.0).astype(BF16)
    t2 = lax.broadcasted_iota(I32, (2 * tq, tq), 0)
    s2 = lax.broadcasted_iota(I32, (2 * tq, tq), 1)
    past = s2 < jnp.where(t2 >= tq, t2 - tq, t2)

    def blocks(j_first, n, diag, o, r):
        rows = [pl.multiple_of((j_first - m) * tq, tq) for m in range(n)]
        zs = [_dot_nt(q2, kn_scr[pl.ds(r0, tq), :]) for r0 in rows]
        ls = [jnp.log2(1.0 + jnp.exp2(-jnp.abs(z))) for z in zs]
        sps = [jnp.maximum(z, 0.0) + l for z, l in zip(zs, ls)]
        if diag:
            sps[0] = jnp.where(past, sps[0], 0.0)
        bts = [_dot(sp.astype(BF16), upper) for sp in sps]
        pv = None
        for m in range(n):
            w = jnp.exp2(jnp.minimum(zs[m], 0.0) - ls[m] - bts[m] - r)
            if diag and m == 0:
                w = jnp.where(past, w, 0.0)
            t = _dot(w.astype(BF16), v_ref[0, pl.ds(rows[m], tq), :])
            pv = t if pv is None else pv + t
            r = r + jnp.sum(sps[m], axis=-1, keepdims=True)
        return o + jnp.where(half0, pv[:tq, :], pv[tq:, :]), r

    def more(c):
        j, _, r = c
        return jnp.logical_and(j >= 0, jnp.min(r) < SB_DONE_AT * LOG2_E)

    def step(c):
        j, o, r = c
        return (j - 1,) + blocks(j, 1, False, o, r)

    zero = (jnp.zeros((tq, LANES), F32), jnp.zeros((2 * tq, 1), F32))

    @pl.when(i < nf - 1)
    def _():
        out = lax.while_loop(more, step, (i - 1,) + blocks(i, 1, True, *zero))
        o_ref[...] = out[1].astype(BF16)

    @pl.when(i >= nf - 1)
    def _():
        out = lax.while_loop(more, step, (i - nf,) + blocks(i, nf, True, *zero))
        o_ref[...] = out[1].astype(BF16)


def _stickbreak(p3, qg, kg, B, S):
    T = B * S
    tq, nf = 256, 2
    nq = S // tq
    npair = BRANCH_W // LANES
    vec = pl.BlockSpec((1, LANES), lambda b, p, i: (0, 0))
    return pl.pallas_call(
        functools.partial(_sb_kernel, S=S, tq=tq, nf=nf),
        grid=(B, npair, nq),
        in_specs=[pl.BlockSpec((1, tq, LANES), lambda b, p, i: (G_SQ + p, b * nq + i, 0)),
                  pl.BlockSpec((1, S, LANES), lambda b, p, i: (G_SK + p, b, 0)),
                  pl.BlockSpec((1, S, LANES), lambda b, p, i: (G_SV + p, b, 0)),
                  vec, vec],
        out_specs=pl.BlockSpec((tq, LANES), lambda b, p, i: (b * nq + i, p)),
        out_shape=jax.ShapeDtypeStruct((T, BRANCH_W), BF16),
        scratch_shapes=[pltpu.VMEM((S, LANES), BF16)],
        compiler_params=_params("parallel", "parallel", "arbitrary"),
        name="stickbreak_attn",
    )(p3, p3, p3, qg, kg)


def _ret_kernel(q_ref, k_ref, v_ref, g_ref, ng_ref, o_ref, st_scr, *, ts):
    C = RT_CHUNK

    @pl.when(pl.program_id(1) == 0)
    def _():
        st_scr[...] = jnp.zeros_like(st_scr)

    rel = (lax.broadcasted_iota(I32, (C, C), 0) - lax.broadcasted_iota(I32, (C, C), 1)).astype(F32)
    pos = lax.broadcasted_iota(I32, (C, LANES), 0).astype(F32)
    ng = ng_ref[...]
    scale = RT_DH ** -0.5
    for h in range(RT_HEADS):
        sl = slice(h * LANES, (h + 1) * LANES)
        lg = math.log1p(-(2.0 ** (-5 - h)))
        dm = jnp.where(rel >= 0, jnp.exp(lg * jnp.maximum(rel, 0.0)), 0.0) * scale
        qd = jnp.exp(lg * (pos + 1.0))
        kd = jnp.exp(lg * (C - 1.0 - pos)) * scale
        cd = math.exp(lg * C)
        for c in range(ts // C):
            rs = slice(c * C, (c + 1) * C)
            q = q_ref[h, rs, :]
            k = k_ref[h, rs, :]
            v = v_ref[h, rs, :]
            intra = _dot((_dot_nt(q, k) * dm).astype(BF16), v)
            st = st_scr[h]
            inter = _dot((q.astype(F32) * qd).astype(BF16), st.astype(BF16))
            st_scr[h] = cd * st + _dot_tn((k.astype(F32) * kd).astype(BF16), v)
            o = intra + inter
            o = o * lax.rsqrt(jnp.mean(o * o, axis=-1, keepdims=True) + EPS) * ng[:, sl]
            gg = g_ref[h, rs, :].astype(F32)
            o_ref[rs, sl] = (o * _silu(gg)).astype(BF16)


def _retention(p3, ng, B, S):
    T = B * S
    ts = 512
    nt = S // ts

    def spec(g0):
        return pl.BlockSpec((RT_HEADS, ts, LANES), lambda b, i: (g0 // RT_HEADS, b * nt + i, 0))

    return pl.pallas_call(
        functools.partial(_ret_kernel, ts=ts),
        grid=(B, nt),
        in_specs=[spec(G_RQ), spec(G_RK), spec(G_RV), spec(G_RG),
                  pl.BlockSpec((1, RT_HEADS * RT_DH), lambda b, i: (0, 0))],
        out_specs=pl.BlockSpec((ts, BRANCH_W), lambda b, i: (b * nt + i, 0)),
        out_shape=jax.ShapeDtypeStruct((T, BRANCH_W), BF16),
        scratch_shapes=[pltpu.VMEM((RT_HEADS, RT_DH, RT_DH), F32)],
        compiler_params=_params("parallel", "arbitrary"),
        name="retention",
    )(p3, p3, p3, p3, ng)


def _merge_kernel(x_ref, o0, o1, o2, o3, ng_ref, sc_ref, sh_ref, ga_ref, wg_ref, bg_ref, wb_ref, wo_ref, o_ref):
    D = D_MODEL
    x = x_ref[...]
    hb = _rms_mod(x, ng_ref[...], sc_ref[0], sh_ref[0]).astype(BF16)
    merged = None
    for n, oref in enumerate((o0, o1, o2, o3)):
        gate = jax.nn.sigmoid(_dot(hb, wg_ref[0, :, n * D:(n + 1) * D]) + bg_ref[:, n * D:(n + 1) * D])
        y = _dot(oref[...], wb_ref[0, n * BRANCH_W:(n + 1) * BRANCH_W, :])
        merged = gate * y if merged is None else merged + gate * y
    out = _dot(merged.astype(BF16), wo_ref[0])
    o_ref[...] = x + ga_ref[0] * out


def _merge(x2, outs, ng, sc, sh, ga, w_gate_b, b_gate, w_branch_b, w_out_b, l, S):
    T, D = x2.shape
    tm = 512
    tpb = S // tm
    mod = pl.BlockSpec((1, 1, D), lambda i: (i // tpb, 0, 0))
    osp = pl.BlockSpec((tm, BRANCH_W), lambda i: (i, 0))
    once = dict(pipeline_mode=pl.Buffered(1))
    return pl.pallas_call(
        _merge_kernel,
        grid=(T // tm,),
        in_specs=[pl.BlockSpec((tm, D), lambda i: (i, 0)), osp, osp, osp, osp,
                  pl.BlockSpec((1, D), lambda i: (0, 0)), mod, mod, mod,
                  pl.BlockSpec((1, D, N_BRANCH * D), lambda i: (l, 0, 0), **once),
                  pl.BlockSpec((1, N_BRANCH * D), lambda i: (0, 0)),
                  pl.BlockSpec((1, N_BRANCH * BRANCH_W, D), lambda i: (l, 0, 0), **once),
                  pl.BlockSpec((1, D, D), lambda i: (l, 0, 0), **once)],
        out_specs=pl.BlockSpec((tm, D), lambda i: (i, 0)),
        out_shape=jax.ShapeDtypeStruct((T, D), F32),
        compiler_params=_params("parallel"),
        name="branch_merge",
    )(x2, *outs, ng, sc, sh, ga, w_gate_b, b_gate, w_branch_b, w_out_b)


def _route_kernel(x_ref, ng_ref, sc_ref, sh_ref, w_ref, b_ref, h_ref, info_ref, tcnt_ref, tbase_ref, cnt_ref,
                  carry_scr, *, tm):
    @pl.when(pl.program_id(0) == 0)
    def _():
        carry_scr[...] = jnp.zeros_like(carry_scr)

    h = _rms_mod(x_ref[...], ng_ref[...], sc_ref[0], sh_ref[0])
    h_ref[...] = h
    w = w_ref[...]
    h_hi, h_mid, _ = _split3(h)
    w_hi, w_mid, _ = _split3(w)
    lg = _dot(h_hi, w_hi) + _dot(h_mid, w_hi) + _dot(h_hi, w_mid) + b_ref[...]

    lane = lax.broadcasted_iota(I32, (tm, LANES), 1)
    lane_f = lane.astype(F32)
    neg = -jnp.inf
    big = float(LANES)

    def first_argmax(vals):
        m = jnp.max(vals, axis=-1, keepdims=True)
        idx = jnp.min(jnp.where(vals == m, lane_f, big), axis=-1, keepdims=True)
        return m, idx

    gmask = lane < N_GROUPS
    gl = jnp.where(gmask, lg, neg)
    gmax, gidx = first_argmax(gl)
    gsum = jnp.sum(jnp.where(gmask, jnp.exp(lg - gmax), 0.0), axis=-1, keepdims=True)
    g_p = 1.0 / gsum
    lo = N_GROUPS + EXP_PER_GROUP * gidx
    emask = (lane_f >= lo) & (lane_f < lo + EXP_PER_GROUP)
    el = jnp.where(emask, lg, neg)
    v1, i1 = first_argmax(el)
    el2 = jnp.where(lane_f == i1, neg, el)
    v2, i2 = first_argmax(el2)
    e2x = jnp.exp(v2 - v1)
    p1 = 1.0 / (1.0 + e2x)
    p2 = e2x * p1
    e1 = i1 - N_GROUPS
    e2 = i2 - N_GROUPS

    onehot = jnp.where((lane_f == e1) | (lane_f == e2), 1.0, 0.0)
    rr = lax.broadcasted_iota(I32, (tm, tm), 0)
    cc = lax.broadcasted_iota(I32, (tm, tm), 1)
    before = jnp.where(cc < rr, 1.0, 0.0).astype(BF16)
    local_rank = _dot(before, onehot.astype(BF16))
    tile_cnt = jnp.sum(onehot, axis=0, keepdims=True)
    run_len = jnp.floor((tile_cnt + (DMA_ROWS - 1)) * (1.0 / DMA_ROWS)) * DMA_ROWS
    er = lax.broadcasted_iota(I32, (LANES, LANES), 0)
    ec = lax.broadcasted_iota(I32, (LANES, LANES), 1)
    lower_e = jnp.where(er < ec, 1.0, 0.0).astype(BF16)
    run_start = _dot(jnp.broadcast_to(run_len, (SUB, LANES)).astype(BF16), lower_e)[0:1, :]
    slot = run_start + local_rank
    pos1 = jnp.sum(jnp.where(lane_f == e1, slot, 0.0), axis=-1, keepdims=True)
    pos2 = jnp.sum(jnp.where(lane_f == e2, slot, 0.0), axis=-1, keepdims=True)

    base = carry_scr[0:1, :]
    total = base + run_len
    carry_scr[...] = jnp.broadcast_to(total, carry_scr.shape)
    cnt_ref[...] = jnp.broadcast_to(total, cnt_ref.shape)
    tcnt_ref[0] = jnp.broadcast_to(run_len * (1.0 / DMA_ROWS), (SUB, LANES))
    tbase_ref[0] = jnp.broadcast_to(base, (SUB, LANES))

    info = jnp.where(lane == 0, e1, 0.0)
    info = jnp.where(lane == 1, e2, info)
    info = jnp.where(lane == 2, g_p * p1, info)
    info = jnp.where(lane == 3, g_p * p2, info)
    info = jnp.where(lane == 4, pos1, info)
    info = jnp.where(lane == 5, pos2, info)
    info_ref[...] = info


def _route(x2, ng, sc, sh, w_r, b_r, S):
    T, D = x2.shape
    tm = MOE_TILE
    tpb = S // tm
    nt = T // tm
    mod = pl.BlockSpec((1, 1, D), lambda i: (i // tpb, 0, 0))
    stat = pl.BlockSpec((1, SUB, LANES), lambda i: (i, 0, 0))
    return pl.pallas_call(
        functools.partial(_route_kernel, tm=tm),
        grid=(nt,),
        in_specs=[pl.BlockSpec((tm, D), lambda i: (i, 0)),
                  pl.BlockSpec((1, D), lambda i: (0, 0)), mod, mod,
                  pl.BlockSpec((D, LANES), lambda i: (0, 0)),
                  pl.BlockSpec((1, LANES), lambda i: (0, 0))],
        out_specs=[pl.BlockSpec((tm, D), lambda i: (i, 0)),
                   pl.BlockSpec((tm, LANES), lambda i: (i, 0)),
                   stat, stat,
                   pl.BlockSpec((SUB, LANES), lambda i: (0, 0))],
        out_shape=[jax.ShapeDtypeStruct((T, D), F32),
                   jax.ShapeDtypeStruct((T, LANES), F32),
                   jax.ShapeDtypeStruct((nt, SUB, LANES), F32),
                   jax.ShapeDtypeStruct((nt, SUB, LANES), F32),
                   jax.ShapeDtypeStruct((SUB, LANES), F32)],
        scratch_shapes=[pltpu.VMEM((SUB, LANES), F32)],
        compiler_params=_params("arbitrary"),
        name="moe_route",
    )(x2, ng, sc, sh, w_r, b_r)


def _run_copies(nch_ref, row_ref, tile, make_copy):
    def per_expert(e, issued):
        nch = nch_ref[tile * N_EXPERTS + e]
        row = row_ref[tile * N_EXPERTS + e]

        def issue(c, z):
            make_copy(pl.multiple_of((issued + c) * DMA_ROWS, DMA_ROWS),
                      pl.multiple_of(row + c * DMA_ROWS, DMA_ROWS)).start()
            return z

        lax.fori_loop(0, nch, issue, 0)
        return issued + nch

    issued = lax.fori_loop(0, N_EXPERTS, per_expert, jnp.int32(0))

    def wait(c, z):
        make_copy(0, 0).wait()
        return z

    lax.fori_loop(0, issued, wait, 0)
    return issued * DMA_ROWS


def _dispatch_kernel(nch_ref, row_ref, tail_n_ref, tail_row_ref, used_ref, h_ref, info_ref, xb_ref, srt, zero, sem,
                     *, tm, n_slot):
    def copy_zero(row):
        return pltpu.make_async_copy(zero.at[pl.ds(0, DMA_ROWS), :],
                                     xb_ref.at[pl.ds(pl.multiple_of(row, DMA_ROWS), DMA_ROWS), :], sem)

    def copy_zero_block(blk):
        return pltpu.make_async_copy(zero, xb_ref.at[pl.ds(pl.multiple_of(blk * MOE_ROWS, MOE_ROWS), MOE_ROWS), :], sem)

    @pl.when(pl.program_id(0) == 0)
    def _():
        zero[...] = jnp.zeros_like(zero)

        def per_expert(e, issued):
            lax.fori_loop(0, tail_n_ref[e], lambda c, z: (copy_zero(tail_row_ref[e] + c * DMA_ROWS).start(), z)[1], 0)
            return issued + tail_n_ref[e]

        issued = lax.fori_loop(0, N_EXPERTS, per_expert, jnp.int32(0))
        lax.fori_loop(0, issued, lambda c, z: (copy_zero(0).wait(), z)[1], 0)
        n_blocks = xb_ref.shape[0] // MOE_ROWS
        lax.fori_loop(used_ref[0], n_blocks, lambda b, z: (copy_zero_block(b).start(), z)[1], 0)
        lax.fori_loop(used_ref[0], n_blocks, lambda b, z: (copy_zero_block(0).wait(), z)[1], 0)

    info_t = info_ref[...].T
    slot = lax.broadcasted_iota(I32, (n_slot, tm), 0).astype(F32)
    perm = jnp.where((slot == info_t[4:5, :]) | (slot == info_t[5:6, :]), 1.0, 0.0).astype(BF16)
    srt[...] = _dot(perm, h_ref[...].astype(BF16))

    def copy(slot_row, buf_row):
        return pltpu.make_async_copy(srt.at[pl.ds(slot_row, DMA_ROWS), :], xb_ref.at[pl.ds(buf_row, DMA_ROWS), :], sem)

    _run_copies(nch_ref, row_ref, pl.program_id(0), copy)


def _dispatch(h2, info, nch_tab, row_tab, tail_n, tail_row, n_used, n_rows):
    T, D = h2.shape
    tm = MOE_TILE
    n_slot = 2 * tm + N_EXPERTS * DMA_ROWS
    grid_spec = pltpu.PrefetchScalarGridSpec(
        num_scalar_prefetch=5,
        grid=(T // tm,),
        in_specs=[pl.BlockSpec((tm, D), lambda i, *_: (i, 0)),
                  pl.BlockSpec((tm, LANES), lambda i, *_: (i, 0))],
        out_specs=pl.BlockSpec(memory_space=pl.ANY),
        scratch_shapes=[pltpu.VMEM((n_slot, D), F32), pltpu.VMEM((MOE_ROWS, D), F32), pltpu.SemaphoreType.DMA(())],
    )
    return pl.pallas_call(
        functools.partial(_dispatch_kernel, tm=tm, n_slot=n_slot),
        grid_spec=grid_spec,
        out_shape=jax.ShapeDtypeStruct((n_rows, D), F32),
        compiler_params=_params("arbitrary"),
        name="moe_dispatch",
    )(nch_tab, row_tab, tail_n, tail_row, n_used, h2, info)


def _expert_kernel(be_ref, used_ref, x_ref, w1_ref, w3_ref, w2_ref, y_ref):
    del be_ref

    @pl.when(pl.program_id(0) < used_ref[0])
    def _():
        x = x_ref[...].astype(BF16)
        a = _silu(_dot(x, w1_ref[0, 0])) * _dot(x, w3_ref[0, 0])
        y_ref[...] = _dot(a.astype(BF16), w2_ref[0, 0])

    @pl.when(pl.program_id(0) >= used_ref[0])
    def _():
        y_ref[...] = jnp.zeros_like(y_ref)


def _experts(xb, blk_e, n_used, w1_b, w3_b, w2_b, l):
    P, D = xb.shape
    R = MOE_ROWS
    grid_spec = pltpu.PrefetchScalarGridSpec(
        num_scalar_prefetch=2,
        grid=(P // R,),
        in_specs=[pl.BlockSpec((R, D), lambda i, be, nu: (i, 0)),
                  pl.BlockSpec((1, 1, D, D_EXPERT), lambda i, be, nu: (l, be[i], 0, 0)),
                  pl.BlockSpec((1, 1, D, D_EXPERT), lambda i, be, nu: (l, be[i], 0, 0)),
                  pl.BlockSpec((1, 1, D_EXPERT, D), lambda i, be, nu: (l, be[i], 0, 0))],
        out_specs=pl.BlockSpec((R, D), lambda i, be, nu: (i, 0)),
    )
    return pl.pallas_call(
        _expert_kernel,
        grid_spec=grid_spec,
        out_shape=jax.ShapeDtypeStruct((P, D), F32),
        compiler_params=_params("arbitrary"),
        name="moe_experts",
    )(blk_e, n_used, xb, w1_b, w3_b, w2_b)


def _combine_kernel(nch_ref, row_ref, x_ref, info_ref, ga_ref, yb_ref, o_ref, srt, sem, *, tm, n_slot):
    def copy(slot_row, buf_row):
        return pltpu.make_async_copy(yb_ref.at[pl.ds(buf_row, DMA_ROWS), :], srt.at[pl.ds(slot_row, DMA_ROWS), :], sem)

    covered = _run_copies(nch_ref, row_ref, pl.program_id(0), copy)

    def clear(r, z):
        srt[pl.ds(pl.multiple_of(r * DMA_ROWS, DMA_ROWS), DMA_ROWS), :] = jnp.zeros((DMA_ROWS, srt.shape[1]), F32)
        return z

    lax.fori_loop(lax.shift_right_logical(covered, DMA_ROWS.bit_length() - 1), n_slot // DMA_ROWS, clear, 0)

    info = info_ref[...]
    lane = lax.broadcasted_iota(I32, (tm, LANES), 1)

    def col(k):
        return jnp.sum(jnp.where(lane == k, info, 0.0), axis=-1, keepdims=True)

    slot = lax.broadcasted_iota(I32, (tm, n_slot), 1).astype(F32)
    gates = jnp.where(slot == col(4), col(2), 0.0) + jnp.where(slot == col(5), col(3), 0.0)
    mixed = _dot(gates.astype(BF16), srt[...].astype(BF16))
    o_ref[...] = x_ref[...] + ga_ref[0] * mixed


def _combine(x2, info, nch_tab, row_tab, ga, yb, S):
    T, D = x2.shape
    tm = MOE_TILE
    tpb = S // tm
    n_slot = 2 * tm + N_EXPERTS * DMA_ROWS
    grid_spec = pltpu.PrefetchScalarGridSpec(
        num_scalar_prefetch=2,
        grid=(T // tm,),
        in_specs=[pl.BlockSpec((tm, D), lambda i, c, d: (i, 0)),
                  pl.BlockSpec((tm, LANES), lambda i, c, d: (i, 0)),
                  pl.BlockSpec((1, 1, D), lambda i, c, d: (i // tpb, 0, 0)),
                  pl.BlockSpec(memory_space=pl.ANY)],
        out_specs=pl.BlockSpec((tm, D), lambda i, c, d: (i, 0)),
        scratch_shapes=[pltpu.VMEM((n_slot, D), F32), pltpu.SemaphoreType.DMA(())],
    )
    return pl.pallas_call(
        functools.partial(_combine_kernel, tm=tm, n_slot=n_slot),
        grid_spec=grid_spec,
        out_shape=jax.ShapeDtypeStruct((T, D), F32),
        compiler_params=_params("arbitrary"),
        name="moe_combine",
    )(nch_tab, row_tab, x2, info, ga, yb)


def _moe(x2, ng, sc, sh, ga, w_r, b_r, w1_b, w3_b, w2_b, l, S):
    T, D = x2.shape
    R = MOE_ROWS
    nt = T // MOE_TILE
    n_rows = (T * 2 + N_EXPERTS * (nt * (DMA_ROWS - 1) + R - 1) + R - 1) // R * R
    h2, info, tnch, tbase, tot = _route(x2, ng, sc, sh, w_r, b_r, S)
    owned = tot[0, :N_EXPERTS].astype(I32)
    padded = (owned + R - 1) // R * R
    pad_end = jnp.cumsum(padded)
    pad_start = pad_end - padded
    nch_tab = tnch[:, 0, :N_EXPERTS].astype(I32).reshape(-1)
    row_tab = (tbase[:, 0, :N_EXPERTS].astype(I32) + pad_start[None, :]).reshape(-1)
    tail_n = (padded - owned) // DMA_ROWS
    tail_row = pad_start + owned
    blk_start = jnp.arange(n_rows // R, dtype=I32) * R
    blk_e = jnp.minimum(jnp.sum((pad_end[None, :] <= blk_start[:, None]).astype(I32), axis=1), N_EXPERTS - 1)
    n_used = (pad_end[-1:] // R).astype(I32)
    xb = _dispatch(h2, info, nch_tab, row_tab, tail_n, tail_row, n_used, n_rows)
    yb = _experts(xb, blk_e, n_used, w1_b, w3_b, w2_b, l)
    return _combine(x2, info, nch_tab, row_tab, ga, yb, S)


def kernel(x, c, ada_w, ada_b, norm1_g, norm2_g, w_in, hgrn_lb, hgrn_norm_g, conv_w, conv_b, conv_ln_g,
           conv_ln_b, sb_qnorm_g, sb_knorm_g, ret_norm_g, w_branch, w_gate, b_gate, w_out, router_group_w,
           router_group_b, router_expert_w, router_expert_b, expert_w1, expert_w3, expert_w2):
    B, S, D = x.shape
    L = ada_w.shape[0]
    T = B * S
    sm = jax.nn.softmax(hgrn_lb.astype(F32), axis=0)
    lower_bounds = jnp.cumsum(sm, axis=0) - sm[0]
    mods = _mods(c, ada_w, ada_b).reshape(L, B, 6, 1, D)
    w_in_b = w_in.astype(BF16)
    w_gate_b = w_gate.astype(BF16)
    w_branch_b = w_branch.astype(BF16)
    w_out_b = w_out.astype(BF16)
    w1_b = expert_w1.astype(BF16)
    w3_b = expert_w3.astype(BF16)
    w2_b = expert_w2.astype(BF16)
    n_r = N_GROUPS + N_EXPERTS
    w_r = jnp.zeros((L, D, LANES), F32).at[:, :, :N_GROUPS].set(router_group_w).at[:, :, N_GROUPS:n_r].set(router_expert_w)
    b_r = jnp.zeros((L, 1, LANES), F32).at[:, 0, :N_GROUPS].set(router_group_b).at[:, 0, N_GROUPS:n_r].set(router_expert_b)

    x2 = x.reshape(T, D)
    for l in range(L):
        sh1, sc1, g1, sh2, sc2, g2 = (mods[l, :, k] for k in range(6))
        n1 = norm1_g[l][None, :]
        p3 = _in_proj(x2, n1, sc1, sh1, w_in_b, l, S)
        o_hg = _hgrn(p3, lower_bounds[l][None, :], hgrn_norm_g[l][None, :], B, S)
        o_cv = _conv(p3, conv_w[l], conv_b[l][None, :], conv_ln_g[l][None, :], conv_ln_b[l][None, :], B, S)
        qg = jnp.tile(sb_qnorm_g[l], 2)[None, :]
        kg = jnp.tile(sb_knorm_g[l], 2)[None, :]
        o_sb = _stickbreak(p3, qg, kg, B, S)
        o_rt = _retention(p3, ret_norm_g[l][None, :], B, S)
        x2 = _merge(x2, (o_hg, o_cv, o_sb, o_rt), n1, sc1, sh1, g1, w_gate_b, b_gate[l][None, :],
                    w_branch_b, w_out_b, l, S)
        x2 = _moe(x2, norm2_g[l][None, :], sc2, sh2, g2, w_r[l], b_r[l], w1_b, w3_b, w2_b, l, S)
    return x2.reshape(B, S, D)
```

```python
import functools
import math

import jax
import jax.numpy as jnp
from jax import lax
from jax.experimental import pallas as pl
from jax.experimental.pallas import tpu as pltpu

F32 = jnp.float32
BF16 = jnp.bfloat16
I32 = jnp.int32

LANES = 128
SUB = 8
VMEM_LIMIT = 56 * 1024 * 1024

D_MODEL = 1024
EPS = 1e-6
LOG_TINY = -87.0
LOG2_E = 1.4426950408889634
HG_HEADS, HG_DK, HG_CHUNK = 4, 128, 16
CV_W, CV_K = 512, 31
CV_HALO = 32
SB_DH = 64
RT_HEADS, RT_DH, RT_CHUNK = 4, 128, 128
N_BRANCH = 4
BRANCH_W = 512
N_GROUPS, EXP_PER_GROUP, N_EXPERTS = 4, 8, 32
D_EXPERT = 512
MOE_ROWS = 256
MOE_TILE = 256
DMA_ROWS = 8
SB_DONE_AT = 104.0

G_HQ, G_HF, G_HI, G_HG = 0, 4, 8, 12
G_CA, G_CG = 16, 20
G_SQ, G_SK, G_SV = 24, 28, 32
G_RQ, G_RK, G_RV, G_RG = 36, 40, 44, 48
N_COL_GROUPS = 52


def _dot(a, b):
    return jnp.dot(a, b, preferred_element_type=F32)


def _dot_nt(a, b):
    return lax.dot_general(a, b, (((1,), (1,)), ((), ())), preferred_element_type=F32)


def _dot_tn(a, b):
    return lax.dot_general(a, b, (((0,), (0,)), ((), ())), preferred_element_type=F32)


def _split3(x):
    hi = x.astype(BF16)
    r1 = x - hi.astype(F32)
    mid = r1.astype(BF16)
    lo = (r1 - mid.astype(F32)).astype(BF16)
    return hi, mid, lo


def _rms_mod(x, g, sc, sh):
    ms = jnp.mean(x * x, axis=-1, keepdims=True)
    return x * lax.rsqrt(ms + EPS) * g * (1.0 + sc) + sh


def _silu(x):
    return x * jax.nn.sigmoid(x)


def _params(*sem):
    return pltpu.CompilerParams(dimension_semantics=sem, vmem_limit_bytes=VMEM_LIMIT)


def _mods_kernel(c_ref, w_ref, b_ref, o_ref):
    c = c_ref[...]
    sc = _silu(c)
    w = w_ref[0]
    acc = None
    for cp in _split3(sc):
        for wp in _split3(w)[:2]:
            t = _dot(cp, wp)
            acc = t if acc is None else acc + t
    o_ref[0] = acc + b_ref[0]


def _mods(c, ada_w, ada_b):
    L, D, E = ada_w.shape
    B = c.shape[0]
    rows = 8
    tn = 1536
    cp = jnp.zeros((rows, D), F32).at[:B].set(c)
    out = pl.pallas_call(
        _mods_kernel,
        grid=(L, E // tn),
        in_specs=[pl.BlockSpec((rows, D), lambda l, j: (0, 0)),
                  pl.BlockSpec((1, D, tn), lambda l, j: (l, 0, j)),
                  pl.BlockSpec((1, 1, tn), lambda l, j: (l, 0, j))],
        out_specs=pl.BlockSpec((1, rows, tn), lambda l, j: (l, 0, j)),
        out_shape=jax.ShapeDtypeStruct((L, rows, E), F32),
        compiler_params=_params("parallel", "parallel"),
        name="adaln_mods",
    )(cp, ada_w, ada_b.reshape(L, 1, E))
    return out[:, :B]


def _in_kernel(x_ref, g_ref, sc_ref, sh_ref, w_ref, o_ref, *, tn):
    hb = _rms_mod(x_ref[...], g_ref[...], sc_ref[0], sh_ref[0]).astype(BF16)
    gpt = tn // LANES
    for j in range(w_ref.shape[2] // tn):
        res = _dot(hb, w_ref[0, :, j * tn:(j + 1) * tn])
        for q in range(gpt):
            o_ref[j * gpt + q] = res[:, q * LANES:(q + 1) * LANES].astype(BF16)


def _in_proj(x2, g, sc, sh, w_in_b, l, S):
    T, D = x2.shape
    tm, tn = 512, 512
    cols = w_in_b.shape[2]
    tpb = S // tm
    mod = pl.BlockSpec((1, 1, D), lambda i: (i // tpb, 0, 0))
    return pl.pallas_call(
        functools.partial(_in_kernel, tn=tn),
        grid=(T // tm,),
        in_specs=[pl.BlockSpec((tm, D), lambda i: (i, 0)),
                  pl.BlockSpec((1, D), lambda i: (0, 0)), mod, mod,
                  pl.BlockSpec((1, D, cols), lambda i: (l, 0, 0), pipeline_mode=pl.Buffered(1))],
        out_specs=pl.BlockSpec((cols // LANES, tm, LANES), lambda i: (0, i, 0)),
        out_shape=jax.ShapeDtypeStruct((cols // LANES, T, LANES), BF16),
        compiler_params=_params("parallel"),
        name="in_proj",
    )(x2, g, sc, sh, w_in_b)


def _hgrn_kernel(q_ref, f_ref, i_ref, g_ref, lb_ref, ng_ref, o_ref, b_scr, c_scr, q_scr, st_scr, *, ts):
    C = HG_CHUNK

    @pl.when(pl.program_id(1) == 0)
    def _():
        st_scr[...] = jnp.zeros_like(st_scr)

    lb = lb_ref[...]
    log_lb = jnp.maximum(jnp.log(jnp.maximum(lb, 1e-30)), LOG_TINY)
    log_1m = jnp.log1p(-lb)
    ng = ng_ref[...]
    rr = lax.broadcasted_iota(I32, (ts, ts), 0)
    cc = lax.broadcasted_iota(I32, (ts, ts), 1)
    shift = C.bit_length() - 1
    tri = jnp.where((jnp.right_shift(rr, shift) == jnp.right_shift(cc, shift)) & (cc <= rr), 1.0, 0.0).astype(BF16)

    for h in range(HG_HEADS):
        sl = slice(h * LANES, (h + 1) * LANES)
        fx = f_ref[h].astype(F32)
        log_gate = log_1m[:, sl] + jnp.minimum(fx, 0.0) - jnp.log(1.0 + jnp.exp(-jnp.abs(fx)))
        a = log_lb[:, sl]
        log_f = jnp.maximum(a, log_gate) + jnp.log(1.0 + jnp.exp(-jnp.abs(a - log_gate)))
        hi, mid, lo = _split3(log_f)
        b2 = (_dot(tri, hi) + _dot(tri, mid) + _dot(tri, lo)) * LOG2_E
        b_scr[h] = b2
        c_scr[h] = b2 - (log_gate - fx) * LOG2_E
        q_scr[h] = _silu(q_ref[h].astype(F32))

    rows = lax.broadcasted_iota(I32, (SUB, LANES), 0)

    def chunk(ci, carry):
        r0 = pl.multiple_of(ci * C, C)
        for h in range(HG_HEADS):
            sl = slice(h * LANES, (h + 1) * LANES)
            b = b_scr[h, pl.ds(r0, C), :]
            c = c_scr[h, pl.ds(r0, C), :]
            q = q_scr[h, pl.ds(r0, C), :]
            v = i_ref[h, pl.ds(r0, C), :].astype(F32)
            b_last = b[C - 1:C, :]
            q_dec = q * jnp.exp2(b)
            k_st = jnp.exp2(b_last - c)
            st = st_scr[h]
            inter = _dot_nt(q_dec.astype(BF16), st.astype(BF16))
            st_scr[h] = st * jnp.exp2(b_last) + _dot_tn(v.astype(BF16), k_st.astype(BF16))
            acc = [jnp.zeros((SUB, LANES), F32) for _ in range(C // SUB)]
            for s in range(C):
                c_s, v_s = c[s:s + 1, :], v[s:s + 1, :]
                for half in range(s // SUB, C // SUB):
                    rs = slice(half * SUB, (half + 1) * SUB)
                    diff = b[rs, :] - c_s
                    if s > half * SUB:
                        diff = jnp.where(rows >= s - half * SUB, diff, -1e30)
                    sc = jnp.sum(q[rs, :] * jnp.exp2(diff), axis=-1, keepdims=True)
                    acc[half] = acc[half] + sc * v_s
            acc = jnp.concatenate(acc, axis=0) + inter
            o = acc * lax.rsqrt(jnp.mean(acc * acc, axis=-1, keepdims=True) + EPS) * ng[:, sl]
            gg = g_ref[h, pl.ds(r0, C), :].astype(F32)
            o_ref[pl.ds(r0, C), sl] = (o * _silu(gg)).astype(BF16)
        return carry

    lax.fori_loop(0, ts // C, chunk, 0)


def _hgrn(p3, lb, ng, B, S):
    T = B * S
    ts = 256
    nt = S // ts

    def spec(g0):
        return pl.BlockSpec((HG_HEADS, ts, LANES), lambda b, i: (g0 // HG_HEADS, b * nt + i, 0))

    return pl.pallas_call(
        functools.partial(_hgrn_kernel, ts=ts),
        grid=(B, nt),
        in_specs=[spec(G_HQ), spec(G_HF), spec(G_HI), spec(G_HG),
                  pl.BlockSpec((1, HG_HEADS * HG_DK), lambda b, i: (0, 0)),
                  pl.BlockSpec((1, HG_HEADS * HG_DK), lambda b, i: (0, 0))],
        out_specs=pl.BlockSpec((ts, BRANCH_W), lambda b, i: (b * nt + i, 0)),
        out_shape=jax.ShapeDtypeStruct((T, BRANCH_W), BF16),
        scratch_shapes=[pltpu.VMEM((HG_HEADS, ts, LANES), F32),
                        pltpu.VMEM((HG_HEADS, ts, LANES), F32),
                        pltpu.VMEM((HG_HEADS, ts, LANES), F32),
                        pltpu.VMEM((HG_HEADS, HG_DK, HG_DK), F32)],
        compiler_params=_params("parallel", "arbitrary"),
        name="hgrn2",
    )(p3, p3, p3, p3, lb, ng)


def _conv_kernel(a_ref, g_ref, w_ref, cb_ref, lg_ref, lbias_ref, o_ref, z_scr, y_scr, *, ts):
    H = CV_HALO

    @pl.when(pl.program_id(1) == 0)
    def _():
        z_scr[0:H, :] = jnp.zeros((H, CV_W), F32)

    @pl.when(pl.program_id(1) > 0)
    def _():
        z_scr[0:H, :] = z_scr[ts:ts + H, :]

    for q in range(CV_W // LANES):
        sl = slice(q * LANES, (q + 1) * LANES)
        z_scr[H:H + ts, sl] = a_ref[q].astype(F32) * jax.nn.sigmoid(g_ref[q].astype(F32))

    off = H - (CV_K - 1)
    for q in range(CV_W // LANES):
        sl = slice(q * LANES, (q + 1) * LANES)
        acc = jnp.zeros((ts, LANES), F32)
        for j in range(CV_K):
            acc = acc + w_ref[j:j + 1, sl] * z_scr[off + j:off + j + ts, sl]
        y_scr[:, sl] = acc + cb_ref[:, sl]

    y = y_scr[...]
    mean = jnp.mean(y, axis=-1, keepdims=True)
    yc = y - mean
    var = jnp.mean(yc * yc, axis=-1, keepdims=True)
    yn = yc * lax.rsqrt(var + EPS) * lg_ref[...] + lbias_ref[...]
    o_ref[...] = _silu(yn).astype(BF16)


def _conv(p3, conv_w, conv_b, ln_g, ln_b, B, S):
    T = B * S
    ts = 256
    nt = S // ts
    ng = CV_W // LANES

    def spec(g0):
        return pl.BlockSpec((ng, ts, LANES), lambda b, i: (g0 // ng, b * nt + i, 0))

    vec = pl.BlockSpec((1, CV_W), lambda b, i: (0, 0))
    return pl.pallas_call(
        functools.partial(_conv_kernel, ts=ts),
        grid=(B, nt),
        in_specs=[spec(G_CA), spec(G_CG), pl.BlockSpec((CV_K, CV_W), lambda b, i: (0, 0)), vec, vec, vec],
        out_specs=pl.BlockSpec((ts, CV_W), lambda b, i: (b * nt + i, 0)),
        out_shape=jax.ShapeDtypeStruct((T, CV_W), BF16),
        scratch_shapes=[pltpu.VMEM((ts + CV_HALO, CV_W), F32), pltpu.VMEM((ts, CV_W), F32)],
        compiler_params=_params("parallel", "arbitrary"),
        name="conformer_conv",
    )(p3, p3, conv_w, conv_b, ln_g, ln_b)


def _sb_kernel(q_ref, k_ref, v_ref, qg_ref, kg_ref, o_ref, kn_scr, *, S, tq, nf, n_sub):
    step = pl.program_id(2)
    lane = lax.broadcasted_iota(I32, (1, LANES), 1)
    half0 = lane < SB_DH

    def head_rms(x, g):
        x2 = x * x
        s0 = jnp.sum(jnp.where(half0, x2, 0.0), axis=-1, keepdims=True)
        s1 = jnp.sum(jnp.where(half0, 0.0, x2), axis=-1, keepdims=True)
        ms = jnp.where(half0, s0, s1) * (1.0 / SB_DH)
        return x * lax.rsqrt(ms + EPS) * g

    kc = 512

    @pl.when(step == 0)
    def _():
        def body(c, carry):
            r0 = pl.multiple_of(c * kc, kc)
            kk = k_ref[0, pl.ds(r0, kc), :].astype(F32)
            kn_scr[pl.ds(r0, kc), :] = head_rms(kk, kg_ref[...]).astype(BF16)
            return carry
        lax.fori_loop(0, S // kc, body, 0)

    for u in range(n_sub):
        _sb_tile(step * n_sub + u, q_ref, v_ref, qg_ref, o_ref, kn_scr, slice(u * tq, (u + 1) * tq),
                 head_rms, half0, tq, nf)


def _sb_tile(i, q_ref, v_ref, qg_ref, o_ref, kn_scr, q_rows, head_rms, half0, tq, nf):
    qn = head_rms(q_ref[0, q_rows, :].astype(F32), qg_ref[...]) * (SB_DH ** -0.5 * LOG2_E)
    q2 = jnp.concatenate([jnp.where(half0, qn, 0.0), jnp.where(half0, 0.0, qn)], axis=0).astype(BF16)
    t_idx = lax.broadcasted_iota(I32, (tq, tq), 0)
    s_idx = lax.broadcasted_iota(I32, (tq, tq), 1)
    upper = jnp.where(t_idx > s_idx, 1.0, 0.0).astype(BF16)
    t2 = lax.broadcasted_iota(I32, (2 * tq, tq), 0)
    s2 = lax.broadcasted_iota(I32, (2 * tq, tq), 1)
    past = s2 < jnp.where(t2 >= tq, t2 - tq, t2)

    def blocks(j_first, n, diag, o, r):
        rows = [pl.multiple_of((j_first - m) * tq, tq) for m in range(n)]
        zs = [_dot_nt(q2, kn_scr[pl.ds(r0, tq), :]) for r0 in rows]
        ls = [jnp.log2(1.0 + jnp.exp2(-jnp.abs(z))) for z in zs]
        sps = [jnp.maximum(z, 0.0) + l for z, l in zip(zs, ls)]
        if diag:
            sps[0] = jnp.where(past, sps[0], 0.0)
        bts = [_dot(sp.astype(BF16), upper) for sp in sps]
        pv = None
        for m in range(n):
            w = jnp.exp2(jnp.minimum(zs[m], 0.0) - ls[m] - bts[m] - r)
            if diag and m == 0:
                w = jnp.where(past, w, 0.0)
            t = _dot(w.astype(BF16), v_ref[0, pl.ds(rows[m], tq), :])
            pv = t if pv is None else pv + t
            r = r + jnp.sum(sps[m], axis=-1, keepdims=True)
        return o + jnp.where(half0, pv[:tq, :], pv[tq:, :]), r

    def more(c):
        j, _, r = c
        return jnp.logical_and(j >= 0, jnp.min(r) < SB_DONE_AT * LOG2_E)

    def step(c):
        j, o, r = c
        return (j - 1,) + blocks(j, 1, False, o, r)

    zero = (jnp.zeros((tq, LANES), F32), jnp.zeros((2 * tq, 1), F32))

    @pl.when(i < nf - 1)
    def _():
        out = lax.while_loop(more, step, (i - 1,) + blocks(i, 1, True, *zero))
        o_ref[q_rows, :] = out[1].astype(BF16)

    @pl.when(i >= nf - 1)
    def _():
        out = lax.while_loop(more, step, (i - nf,) + blocks(i, nf, True, *zero))
        o_ref[q_rows, :] = out[1].astype(BF16)


def _stickbreak(p3, qg, kg, B, S):
    T = B * S
    tq, nf, n_sub = 256, 2, 2
    nq = S // (tq * n_sub)
    npair = BRANCH_W // LANES
    vec = pl.BlockSpec((1, LANES), lambda b, p, i: (0, 0))
    return pl.pallas_call(
        functools.partial(_sb_kernel, S=S, tq=tq, nf=nf, n_sub=n_sub),
        grid=(B, npair, nq),
        in_specs=[pl.BlockSpec((1, tq * n_sub, LANES), lambda b, p, i: (G_SQ + p, b * nq + i, 0)),
                  pl.BlockSpec((1, S, LANES), lambda b, p, i: (G_SK + p, b, 0)),
                  pl.BlockSpec((1, S, LANES), lambda b, p, i: (G_SV + p, b, 0)),
                  vec, vec],
        out_specs=pl.BlockSpec((tq * n_sub, LANES), lambda b, p, i: (b * nq + i, p)),
        out_shape=jax.ShapeDtypeStruct((T, BRANCH_W), BF16),
        scratch_shapes=[pltpu.VMEM((S, LANES), BF16)],
        compiler_params=_params("parallel", "parallel", "arbitrary"),
        name="stickbreak_attn",
    )(p3, p3, p3, qg, kg)


def _ret_kernel(q_ref, k_ref, v_ref, g_ref, ng_ref, o_ref, st_scr, *, ts):
    C = RT_CHUNK

    @pl.when(pl.program_id(1) == 0)
    def _():
        st_scr[...] = jnp.zeros_like(st_scr)

    rel = (lax.broadcasted_iota(I32, (C, C), 0) - lax.broadcasted_iota(I32, (C, C), 1)).astype(F32)
    pos = lax.broadcasted_iota(I32, (C, LANES), 0).astype(F32)
    ng = ng_ref[...]
    scale = RT_DH ** -0.5
    for h in range(RT_HEADS):
        sl = slice(h * LANES, (h + 1) * LANES)
        lg = math.log1p(-(2.0 ** (-5 - h)))
        dm = jnp.where(rel >= 0, jnp.exp(lg * jnp.maximum(rel, 0.0)), 0.0) * scale
        qd = jnp.exp(lg * (pos + 1.0))
        kd = jnp.exp(lg * (C - 1.0 - pos)) * scale
        cd = math.exp(lg * C)
        for c in range(ts // C):
            rs = slice(c * C, (c + 1) * C)
            q = q_ref[h, rs, :]
            k = k_ref[h, rs, :]
            v = v_ref[h, rs, :]
            intra = _dot((_dot_nt(q, k) * dm).astype(BF16), v)
            st = st_scr[h]
            inter = _dot((q.astype(F32) * qd).astype(BF16), st.astype(BF16))
            st_scr[h] = cd * st + _dot_tn((k.astype(F32) * kd).astype(BF16), v)
            o = intra + inter
            o = o * lax.rsqrt(jnp.mean(o * o, axis=-1, keepdims=True) + EPS) * ng[:, sl]
            gg = g_ref[h, rs, :].astype(F32)
            o_ref[rs, sl] = (o * _silu(gg)).astype(BF16)


def _retention(p3, ng, B, S):
    T = B * S
    ts = 512
    nt = S // ts

    def spec(g0):
        return pl.BlockSpec((RT_HEADS, ts, LANES), lambda b, i: (g0 // RT_HEADS, b * nt + i, 0))

    return pl.pallas_call(
        functools.partial(_ret_kernel, ts=ts),
        grid=(B, nt),
        in_specs=[spec(G_RQ), spec(G_RK), spec(G_RV), spec(G_RG),
                  pl.BlockSpec((1, RT_HEADS * RT_DH), lambda b, i: (0, 0))],
        out_specs=pl.BlockSpec((ts, BRANCH_W), lambda b, i: (b * nt + i, 0)),
        out_shape=jax.ShapeDtypeStruct((T, BRANCH_W), BF16),
        scratch_shapes=[pltpu.VMEM((RT_HEADS, RT_DH, RT_DH), F32)],
        compiler_params=_params("parallel", "arbitrary"),
        name="retention",
    )(p3, p3, p3, p3, ng)


def _merge_kernel(x_ref, o0, o1, o2, o3, ng_ref, sc_ref, sh_ref, ga_ref, wg_ref, bg_ref, wb_ref, wo_ref, o_ref):
    D = D_MODEL
    x = x_ref[...]
    hb = _rms_mod(x, ng_ref[...], sc_ref[0], sh_ref[0]).astype(BF16)
    merged = None
    for n, oref in enumerate((o0, o1, o2, o3)):
        gate = jax.nn.sigmoid(_dot(hb, wg_ref[0, :, n * D:(n + 1) * D]) + bg_ref[:, n * D:(n + 1) * D])
        y = _dot(oref[...], wb_ref[0, n * BRANCH_W:(n + 1) * BRANCH_W, :])
        merged = gate * y if merged is None else merged + gate * y
    out = _dot(merged.astype(BF16), wo_ref[0])
    o_ref[...] = x + ga_ref[0] * out


def _merge(x2, outs, ng, sc, sh, ga, w_gate_b, b_gate, w_branch_b, w_out_b, l, S):
    T, D = x2.shape
    tm = 512
    tpb = S // tm
    mod = pl.BlockSpec((1, 1, D), lambda i: (i // tpb, 0, 0))
    osp = pl.BlockSpec((tm, BRANCH_W), lambda i: (i, 0))
    once = dict(pipeline_mode=pl.Buffered(1))
    return pl.pallas_call(
        _merge_kernel,
        grid=(T // tm,),
        in_specs=[pl.BlockSpec((tm, D), lambda i: (i, 0)), osp, osp, osp, osp,
                  pl.BlockSpec((1, D), lambda i: (0, 0)), mod, mod, mod,
                  pl.BlockSpec((1, D, N_BRANCH * D), lambda i: (l, 0, 0), **once),
                  pl.BlockSpec((1, N_BRANCH * D), lambda i: (0, 0)),
                  pl.BlockSpec((1, N_BRANCH * BRANCH_W, D), lambda i: (l, 0, 0), **once),
                  pl.BlockSpec((1, D, D), lambda i: (l, 0, 0), **once)],
        out_specs=pl.BlockSpec((tm, D), lambda i: (i, 0)),
        out_shape=jax.ShapeDtypeStruct((T, D), F32),
        compiler_params=_params("parallel"),
        name="branch_merge",
    )(x2, *outs, ng, sc, sh, ga, w_gate_b, b_gate, w_branch_b, w_out_b)


def _route_kernel(x_ref, ng_ref, sc_ref, sh_ref, w_ref, b_ref, h_ref, info_ref, tcnt_ref, tbase_ref, cnt_ref,
                  carry_scr, *, tm):
    @pl.when(pl.program_id(0) == 0)
    def _():
        carry_scr[...] = jnp.zeros_like(carry_scr)

    h = _rms_mod(x_ref[...], ng_ref[...], sc_ref[0], sh_ref[0])
    h_ref[...] = h
    w = w_ref[...]
    h_hi, h_mid, _ = _split3(h)
    w_hi, w_mid, _ = _split3(w)
    lg = _dot(h_hi, w_hi) + _dot(h_mid, w_hi) + _dot(h_hi, w_mid) + b_ref[...]

    lane = lax.broadcasted_iota(I32, (tm, LANES), 1)
    lane_f = lane.astype(F32)
    neg = -jnp.inf
    big = float(LANES)

    def first_argmax(vals):
        m = jnp.max(vals, axis=-1, keepdims=True)
        idx = jnp.min(jnp.where(vals == m, lane_f, big), axis=-1, keepdims=True)
        return m, idx

    gmask = lane < N_GROUPS
    gl = jnp.where(gmask, lg, neg)
    gmax, gidx = first_argmax(gl)
    gsum = jnp.sum(jnp.where(gmask, jnp.exp(lg - gmax), 0.0), axis=-1, keepdims=True)
    g_p = 1.0 / gsum
    lo = N_GROUPS + EXP_PER_GROUP * gidx
    emask = (lane_f >= lo) & (lane_f < lo + EXP_PER_GROUP)
    el = jnp.where(emask, lg, neg)
    v1, i1 = first_argmax(el)
    el2 = jnp.where(lane_f == i1, neg, el)
    v2, i2 = first_argmax(el2)
    e2x = jnp.exp(v2 - v1)
    p1 = 1.0 / (1.0 + e2x)
    p2 = e2x * p1
    e1 = i1 - N_GROUPS
    e2 = i2 - N_GROUPS

    onehot = jnp.where((lane_f == e1) | (lane_f == e2), 1.0, 0.0)
    rr = lax.broadcasted_iota(I32, (tm, tm), 0)
    cc = lax.broadcasted_iota(I32, (tm, tm), 1)
    before = jnp.where(cc < rr, 1.0, 0.0).astype(BF16)
    local_rank = _dot(before, onehot.astype(BF16))
    tile_cnt = jnp.sum(onehot, axis=0, keepdims=True)
    run_len = jnp.floor((tile_cnt + (DMA_ROWS - 1)) * (1.0 / DMA_ROWS)) * DMA_ROWS
    er = lax.broadcasted_iota(I32, (LANES, LANES), 0)
    ec = lax.broadcasted_iota(I32, (LANES, LANES), 1)
    lower_e = jnp.where(er < ec, 1.0, 0.0).astype(BF16)
    run_start = _dot(jnp.broadcast_to(run_len, (SUB, LANES)).astype(BF16), lower_e)[0:1, :]
    slot = run_start + local_rank
    pos1 = jnp.sum(jnp.where(lane_f == e1, slot, 0.0), axis=-1, keepdims=True)
    pos2 = jnp.sum(jnp.where(lane_f == e2, slot, 0.0), axis=-1, keepdims=True)

    base = carry_scr[0:1, :]
    total = base + run_len
    carry_scr[...] = jnp.broadcast_to(total, carry_scr.shape)
    cnt_ref[...] = jnp.broadcast_to(total, cnt_ref.shape)
    tcnt_ref[0] = jnp.broadcast_to(run_len * (1.0 / DMA_ROWS), (SUB, LANES))
    tbase_ref[0] = jnp.broadcast_to(base, (SUB, LANES))

    info = jnp.where(lane == 0, e1, 0.0)
    info = jnp.where(lane == 1, e2, info)
    info = jnp.where(lane == 2, g_p * p1, info)
    info = jnp.where(lane == 3, g_p * p2, info)
    info = jnp.where(lane == 4, pos1, info)
    info = jnp.where(lane == 5, pos2, info)
    info_ref[...] = info


def _route(x2, ng, sc, sh, w_r, b_r, S):
    T, D = x2.shape
    tm = MOE_TILE
    tpb = S // tm
    nt = T // tm
    mod = pl.BlockSpec((1, 1, D), lambda i: (i // tpb, 0, 0))
    stat = pl.BlockSpec((1, SUB, LANES), lambda i: (i, 0, 0))
    return pl.pallas_call(
        functools.partial(_route_kernel, tm=tm),
        grid=(nt,),
        in_specs=[pl.BlockSpec((tm, D), lambda i: (i, 0)),
                  pl.BlockSpec((1, D), lambda i: (0, 0)), mod, mod,
                  pl.BlockSpec((D, LANES), lambda i: (0, 0)),
                  pl.BlockSpec((1, LANES), lambda i: (0, 0))],
        out_specs=[pl.BlockSpec((tm, D), lambda i: (i, 0)),
                   pl.BlockSpec((tm, LANES), lambda i: (i, 0)),
                   stat, stat,
                   pl.BlockSpec((SUB, LANES), lambda i: (0, 0))],
        out_shape=[jax.ShapeDtypeStruct((T, D), F32),
                   jax.ShapeDtypeStruct((T, LANES), F32),
                   jax.ShapeDtypeStruct((nt, SUB, LANES), F32),
                   jax.ShapeDtypeStruct((nt, SUB, LANES), F32),
                   jax.ShapeDtypeStruct((SUB, LANES), F32)],
        scratch_shapes=[pltpu.VMEM((SUB, LANES), F32)],
        compiler_params=_params("arbitrary"),
        name="moe_route",
    )(x2, ng, sc, sh, w_r, b_r)


def _start_runs(nch_ref, row_ref, tile, make_copy):
    def per_expert(e, issued):
        nch = nch_ref[tile * N_EXPERTS + e]
        row = row_ref[tile * N_EXPERTS + e]

        def issue(c, z):
            make_copy(pl.multiple_of((issued + c) * DMA_ROWS, DMA_ROWS),
                      pl.multiple_of(row + c * DMA_ROWS, DMA_ROWS)).start()
            return z

        lax.fori_loop(0, nch, issue, 0)
        return issued + nch

    return lax.fori_loop(0, N_EXPERTS, per_expert, jnp.int32(0))


def _wait_runs(count, make_copy):
    lax.fori_loop(0, count, lambda c, z: (make_copy(0, 0).wait(), z)[1], 0)


def _dispatch_kernel(nch_ref, row_ref, tail_n_ref, tail_row_ref, used_ref, h_ref, info_ref, xb_ref, srt, zero, sems,
                     pending, *, tm, n_slot):
    i = pl.program_id(0)
    cur = lax.rem(i, 2)
    sem = sems.at[0]

    def copy_zero(row):
        return pltpu.make_async_copy(zero.at[pl.ds(0, DMA_ROWS), :],
                                     xb_ref.at[pl.ds(pl.multiple_of(row, DMA_ROWS), DMA_ROWS), :], sem)

    def copy_zero_block(blk):
        return pltpu.make_async_copy(zero, xb_ref.at[pl.ds(pl.multiple_of(blk * MOE_ROWS, MOE_ROWS), MOE_ROWS), :], sem)

    @pl.when(i == 0)
    def _():
        zero[...] = jnp.zeros_like(zero)

        def per_expert(e, issued):
            lax.fori_loop(0, tail_n_ref[e], lambda c, z: (copy_zero(tail_row_ref[e] + c * DMA_ROWS).start(), z)[1], 0)
            return issued + tail_n_ref[e]

        issued = lax.fori_loop(0, N_EXPERTS, per_expert, jnp.int32(0))
        lax.fori_loop(0, issued, lambda c, z: (copy_zero(0).wait(), z)[1], 0)
        n_blocks = xb_ref.shape[0] // MOE_ROWS
        lax.fori_loop(used_ref[0], n_blocks, lambda b, z: (copy_zero_block(b).start(), z)[1], 0)
        lax.fori_loop(used_ref[0], n_blocks, lambda b, z: (copy_zero_block(0).wait(), z)[1], 0)

    info_t = info_ref[...].T
    slot = lax.broadcasted_iota(I32, (n_slot, tm), 0).astype(F32)
    perm = jnp.where((slot == info_t[4:5, :]) | (slot == info_t[5:6, :]), 1.0, 0.0).astype(BF16)
    srt[cur] = _dot(perm, h_ref[...].astype(BF16))

    def copier(buf):
        def copy(slot_row, buf_row):
            return pltpu.make_async_copy(srt.at[buf, pl.ds(slot_row, DMA_ROWS), :],
                                         xb_ref.at[pl.ds(buf_row, DMA_ROWS), :], sems.at[buf])
        return copy

    @pl.when(i > 0)
    def _():
        _wait_runs(pending[0], copier(1 - cur))

    pending[0] = _start_runs(nch_ref, row_ref, i, copier(cur))

    @pl.when(i == pl.num_programs(0) - 1)
    def _():
        _wait_runs(pending[0], copier(cur))


def _dispatch(h2, info, nch_tab, row_tab, tail_n, tail_row, n_used, n_rows):
    T, D = h2.shape
    tm = MOE_TILE
    n_slot = 2 * tm + N_EXPERTS * DMA_ROWS
    grid_spec = pltpu.PrefetchScalarGridSpec(
        num_scalar_prefetch=5,
        grid=(T // tm,),
        in_specs=[pl.BlockSpec((tm, D), lambda i, *_: (i, 0)),
                  pl.BlockSpec((tm, LANES), lambda i, *_: (i, 0))],
        out_specs=pl.BlockSpec(memory_space=pl.ANY),
        scratch_shapes=[pltpu.VMEM((2, n_slot, D), F32), pltpu.VMEM((MOE_ROWS, D), F32),
                        pltpu.SemaphoreType.DMA((2,)), pltpu.SMEM((1,), I32)],
    )
    return pl.pallas_call(
        functools.partial(_dispatch_kernel, tm=tm, n_slot=n_slot),
        grid_spec=grid_spec,
        out_shape=jax.ShapeDtypeStruct((n_rows, D), F32),
        compiler_params=_params("arbitrary"),
        name="moe_dispatch",
    )(nch_tab, row_tab, tail_n, tail_row, n_used, h2, info)


def _expert_kernel(be_ref, used_ref, x_ref, w1_ref, w3_ref, w2_ref, y_ref, w1_b, w3_b, w2_b):
    i = pl.program_id(0)
    live = i < used_ref[0]

    @pl.when(live & ((i == 0) | (be_ref[i] != be_ref[jnp.maximum(i - 1, 0)])))
    def _():
        w1_b[...] = w1_ref[0, 0].astype(BF16)
        w3_b[...] = w3_ref[0, 0].astype(BF16)
        w2_b[...] = w2_ref[0, 0].astype(BF16)

    @pl.when(live)
    def _():
        x = x_ref[...].astype(BF16)
        a = _silu(_dot(x, w1_b[...])) * _dot(x, w3_b[...])
        y_ref[...] = _dot(a.astype(BF16), w2_b[...])

    @pl.when(jnp.logical_not(live))
    def _():
        y_ref[...] = jnp.zeros_like(y_ref)


def _experts(xb, blk_e, n_used, w1, w3, w2, l):
    P, D = xb.shape
    R = MOE_ROWS
    grid_spec = pltpu.PrefetchScalarGridSpec(
        num_scalar_prefetch=2,
        grid=(P // R,),
        in_specs=[pl.BlockSpec((R, D), lambda i, be, nu: (i, 0)),
                  pl.BlockSpec((1, 1, D, D_EXPERT), lambda i, be, nu: (l, be[i], 0, 0)),
                  pl.BlockSpec((1, 1, D, D_EXPERT), lambda i, be, nu: (l, be[i], 0, 0)),
                  pl.BlockSpec((1, 1, D_EXPERT, D), lambda i, be, nu: (l, be[i], 0, 0))],
        out_specs=pl.BlockSpec((R, D), lambda i, be, nu: (i, 0)),
        scratch_shapes=[pltpu.VMEM((D, D_EXPERT), BF16), pltpu.VMEM((D, D_EXPERT), BF16),
                        pltpu.VMEM((D_EXPERT, D), BF16)],
    )
    return pl.pallas_call(
        _expert_kernel,
        grid_spec=grid_spec,
        out_shape=jax.ShapeDtypeStruct((P, D), F32),
        compiler_params=_params("arbitrary"),
        name="moe_experts",
    )(blk_e, n_used, xb, w1, w3, w2)


def _combine_kernel(nch_ref, row_ref, x_ref, info_ref, ga_ref, yb_ref, o_ref, srt, sems, pending, *, tm, n_slot):
    i = pl.program_id(0)
    cur = lax.rem(i, 2)

    def copier(buf):
        def copy(slot_row, buf_row):
            return pltpu.make_async_copy(yb_ref.at[pl.ds(buf_row, DMA_ROWS), :],
                                         srt.at[buf, pl.ds(slot_row, DMA_ROWS), :], sems.at[buf])
        return copy

    @pl.when(i == 0)
    def _():
        pending[0] = _start_runs(nch_ref, row_ref, 0, copier(0))

    @pl.when(i + 1 < pl.num_programs(0))
    def _():
        pending[1 - cur] = _start_runs(nch_ref, row_ref, i + 1, copier(1 - cur))

    covered = pending[cur]
    _wait_runs(covered, copier(cur))

    def clear(r, z):
        srt[cur, pl.ds(pl.multiple_of(r * DMA_ROWS, DMA_ROWS), DMA_ROWS), :] = jnp.zeros((DMA_ROWS, srt.shape[2]), F32)
        return z

    lax.fori_loop(covered, n_slot // DMA_ROWS, clear, 0)

    info = info_ref[...]
    lane = lax.broadcasted_iota(I32, (tm, LANES), 1)

    def col(k):
        return jnp.sum(jnp.where(lane == k, info, 0.0), axis=-1, keepdims=True)

    slot = lax.broadcasted_iota(I32, (tm, n_slot), 1).astype(F32)
    gates = jnp.where(slot == col(4), col(2), 0.0) + jnp.where(slot == col(5), col(3), 0.0)
    mixed = _dot(gates.astype(BF16), srt[cur].astype(BF16))
    o_ref[...] = x_ref[...] + ga_ref[0] * mixed


def _combine(x2, info, nch_tab, row_tab, ga, yb, S):
    T, D = x2.shape
    tm = MOE_TILE
    tpb = S // tm
    n_slot = 2 * tm + N_EXPERTS * DMA_ROWS
    grid_spec = pltpu.PrefetchScalarGridSpec(
        num_scalar_prefetch=2,
        grid=(T // tm,),
        in_specs=[pl.BlockSpec((tm, D), lambda i, c, d: (i, 0)),
                  pl.BlockSpec((tm, LANES), lambda i, c, d: (i, 0)),
                  pl.BlockSpec((1, 1, D), lambda i, c, d: (i // tpb, 0, 0)),
                  pl.BlockSpec(memory_space=pl.ANY)],
        out_specs=pl.BlockSpec((tm, D), lambda i, c, d: (i, 0)),
        scratch_shapes=[pltpu.VMEM((2, n_slot, D), F32), pltpu.SemaphoreType.DMA((2,)), pltpu.SMEM((2,), I32)],
    )
    return pl.pallas_call(
        functools.partial(_combine_kernel, tm=tm, n_slot=n_slot),
        grid_spec=grid_spec,
        out_shape=jax.ShapeDtypeStruct((T, D), F32),
        compiler_params=_params("arbitrary"),
        name="moe_combine",
    )(nch_tab, row_tab, x2, info, ga, yb)


def _moe(x2, ng, sc, sh, ga, w_r, b_r, w1, w3, w2, l, S):
    T, D = x2.shape
    R = MOE_ROWS
    nt = T // MOE_TILE
    n_rows = (T * 2 + N_EXPERTS * (nt * (DMA_ROWS - 1) + R - 1) + R - 1) // R * R
    h2, info, tnch, tbase, tot = _route(x2, ng, sc, sh, w_r, b_r, S)
    owned = tot[0, :N_EXPERTS].astype(I32)
    padded = (owned + R - 1) // R * R
    pad_end = jnp.cumsum(padded)
    pad_start = pad_end - padded
    nch_tab = tnch[:, 0, :N_EXPERTS].astype(I32).reshape(-1)
    row_tab = (tbase[:, 0, :N_EXPERTS].astype(I32) + pad_start[None, :]).reshape(-1)
    tail_n = (padded - owned) // DMA_ROWS
    tail_row = pad_start + owned
    blk_start = jnp.arange(n_rows // R, dtype=I32) * R
    blk_e = jnp.minimum(jnp.sum((pad_end[None, :] <= blk_start[:, None]).astype(I32), axis=1), N_EXPERTS - 1)
    n_used = (pad_end[-1:] // R).astype(I32)
    xb = _dispatch(h2, info, nch_tab, row_tab, tail_n, tail_row, n_used, n_rows)
    yb = _experts(xb, blk_e, n_used, w1, w3, w2, l)
    return _combine(x2, info, nch_tab, row_tab, ga, yb, S)


def kernel(x, c, ada_w, ada_b, norm1_g, norm2_g, w_in, hgrn_lb, hgrn_norm_g, conv_w, conv_b, conv_ln_g,
           conv_ln_b, sb_qnorm_g, sb_knorm_g, ret_norm_g, w_branch, w_gate, b_gate, w_out, router_group_w,
           router_group_b, router_expert_w, router_expert_b, expert_w1, expert_w3, expert_w2):
    B, S, D = x.shape
    L = ada_w.shape[0]
    T = B * S
    sm = jax.nn.softmax(hgrn_lb.astype(F32), axis=0)
    lower_bounds = jnp.cumsum(sm, axis=0) - sm[0]
    mods = _mods(c, ada_w, ada_b).reshape(L, B, 6, 1, D)
    w_in_b = w_in.astype(BF16)
    w_gate_b = w_gate.astype(BF16)
    w_branch_b = w_branch.astype(BF16)
    w_out_b = w_out.astype(BF16)
    n_r = N_GROUPS + N_EXPERTS
    w_r = jnp.zeros((L, D, LANES), F32).at[:, :, :N_GROUPS].set(router_group_w).at[:, :, N_GROUPS:n_r].set(router_expert_w)
    b_r = jnp.zeros((L, 1, LANES), F32).at[:, 0, :N_GROUPS].set(router_group_b).at[:, 0, N_GROUPS:n_r].set(router_expert_b)

    x2 = x.reshape(T, D)
    for l in range(L):
        sh1, sc1, g1, sh2, sc2, g2 = (mods[l, :, k] for k in range(6))
        n1 = norm1_g[l][None, :]
        p3 = _in_proj(x2, n1, sc1, sh1, w_in_b, l, S)
        o_hg = _hgrn(p3, lower_bounds[l][None, :], hgrn_norm_g[l][None, :], B, S)
        o_cv = _conv(p3, conv_w[l], conv_b[l][None, :], conv_ln_g[l][None, :], conv_ln_b[l][None, :], B, S)
        qg = jnp.tile(sb_qnorm_g[l], 2)[None, :]
        kg = jnp.tile(sb_knorm_g[l], 2)[None, :]
        o_sb = _stickbreak(p3, qg, kg, B, S)
        o_rt = _retention(p3, ret_norm_g[l][None, :], B, S)
        x2 = _merge(x2, (o_hg, o_cv, o_sb, o_rt), n1, sc1, sh1, g1, w_gate_b, b_gate[l][None, :],
                    w_branch_b, w_out_b, l, S)
        x2 = _moe(x2, norm2_g[l][None, :], sc2, sh2, g2, w_r[l], b_r[l], expert_w1, expert_w3, expert_w2, l, S)
    return x2.reshape(B, S, D)
```

```python
import functools
import math

import jax
import jax.numpy as jnp
from jax import lax
from jax.experimental import pallas as pl
from jax.experimental.pallas import tpu as pltpu

F32 = jnp.float32
BF16 = jnp.bfloat16
I32 = jnp.int32

LANES = 128
SUB = 8
VMEM_LIMIT = 56 * 1024 * 1024

D_MODEL = 1024
EPS = 1e-6
LOG_TINY = -87.0
LOG2_E = 1.4426950408889634
HG_HEADS, HG_DK, HG_CHUNK = 4, 128, 16
CV_W, CV_K = 512, 31
CV_HALO = 32
SB_DH = 64
RT_HEADS, RT_DH, RT_CHUNK = 4, 128, 128
N_BRANCH = 4
BRANCH_W = 512
N_GROUPS, EXP_PER_GROUP, N_EXPERTS = 4, 8, 32
D_EXPERT = 512
MOE_ROWS = 512
MOE_TILE = 256
DMA_ROWS = 8
SB_DONE_AT = 104.0

G_HQ, G_HF, G_HI, G_HG = 0, 4, 8, 12
G_CA, G_CG = 16, 20
G_SQ, G_SK, G_SV = 24, 28, 32
G_RQ, G_RK, G_RV, G_RG = 36, 40, 44, 48
N_COL_GROUPS = 52


def _dot(a, b):
    return jnp.dot(a, b, preferred_element_type=F32)


def _dot_nt(a, b):
    return lax.dot_general(a, b, (((1,), (1,)), ((), ())), preferred_element_type=F32)


def _dot_tn(a, b):
    return lax.dot_general(a, b, (((0,), (0,)), ((), ())), preferred_element_type=F32)


def _split3(x):
    hi = x.astype(BF16)
    r1 = x - hi.astype(F32)
    mid = r1.astype(BF16)
    lo = (r1 - mid.astype(F32)).astype(BF16)
    return hi, mid, lo


def _rms_mod(x, g, sc, sh):
    ms = jnp.mean(x * x, axis=-1, keepdims=True)
    return x * lax.rsqrt(ms + EPS) * g * (1.0 + sc) + sh


def _silu(x):
    return x * jax.nn.sigmoid(x)


def _params(*sem):
    return pltpu.CompilerParams(dimension_semantics=sem, vmem_limit_bytes=VMEM_LIMIT)


def _mods_kernel(c_ref, w_ref, b_ref, o_ref):
    c = c_ref[...]
    sc = _silu(c)
    w = w_ref[0]
    acc = None
    for cp in _split3(sc):
        for wp in _split3(w)[:2]:
            t = _dot(cp, wp)
            acc = t if acc is None else acc + t
    o_ref[0] = acc + b_ref[0]


def _mods(c, ada_w, ada_b):
    L, D, E = ada_w.shape
    B = c.shape[0]
    rows = 8
    tn = 1536
    cp = jnp.zeros((rows, D), F32).at[:B].set(c)
    out = pl.pallas_call(
        _mods_kernel,
        grid=(L, E // tn),
        in_specs=[pl.BlockSpec((rows, D), lambda l, j: (0, 0)),
                  pl.BlockSpec((1, D, tn), lambda l, j: (l, 0, j)),
                  pl.BlockSpec((1, 1, tn), lambda l, j: (l, 0, j))],
        out_specs=pl.BlockSpec((1, rows, tn), lambda l, j: (l, 0, j)),
        out_shape=jax.ShapeDtypeStruct((L, rows, E), F32),
        compiler_params=_params("parallel", "parallel"),
        name="adaln_mods",
    )(cp, ada_w, ada_b.reshape(L, 1, E))
    return out[:, :B]


def _in_kernel(x_ref, g_ref, sc_ref, sh_ref, w_ref, o_ref, *, tn):
    hb = _rms_mod(x_ref[...], g_ref[...], sc_ref[0], sh_ref[0]).astype(BF16)
    gpt = tn // LANES
    for j in range(w_ref.shape[2] // tn):
        res = _dot(hb, w_ref[0, :, j * tn:(j + 1) * tn])
        for q in range(gpt):
            o_ref[j * gpt + q] = res[:, q * LANES:(q + 1) * LANES].astype(BF16)


def _in_proj(x2, g, sc, sh, w_in_b, l, S):
    T, D = x2.shape
    tm, tn = 512, 512
    cols = w_in_b.shape[2]
    tpb = S // tm
    mod = pl.BlockSpec((1, 1, D), lambda i: (i // tpb, 0, 0))
    return pl.pallas_call(
        functools.partial(_in_kernel, tn=tn),
        grid=(T // tm,),
        in_specs=[pl.BlockSpec((tm, D), lambda i: (i, 0)),
                  pl.BlockSpec((1, D), lambda i: (0, 0)), mod, mod,
                  pl.BlockSpec((1, D, cols), lambda i: (l, 0, 0), pipeline_mode=pl.Buffered(1))],
        out_specs=pl.BlockSpec((cols // LANES, tm, LANES), lambda i: (0, i, 0)),
        out_shape=jax.ShapeDtypeStruct((cols // LANES, T, LANES), BF16),
        compiler_params=_params("parallel"),
        name="in_proj",
    )(x2, g, sc, sh, w_in_b)


def _hgrn_kernel(q_ref, f_ref, i_ref, g_ref, lb_ref, ng_ref, o_ref, b_scr, c_scr, q_scr, st_scr, *, ts):
    C = HG_CHUNK

    @pl.when(pl.program_id(1) == 0)
    def _():
        st_scr[...] = jnp.zeros_like(st_scr)

    lb = lb_ref[...]
    log_lb = jnp.maximum(jnp.log(jnp.maximum(lb, 1e-30)), LOG_TINY)
    log_1m = jnp.log1p(-lb)
    ng = ng_ref[...]
    rr = lax.broadcasted_iota(I32, (ts, ts), 0)
    cc = lax.broadcasted_iota(I32, (ts, ts), 1)
    shift = C.bit_length() - 1
    tri = jnp.where((jnp.right_shift(rr, shift) == jnp.right_shift(cc, shift)) & (cc <= rr), 1.0, 0.0).astype(BF16)

    for h in range(HG_HEADS):
        sl = slice(h * LANES, (h + 1) * LANES)
        fx = f_ref[h].astype(F32)
        log_gate = log_1m[:, sl] + jnp.minimum(fx, 0.0) - jnp.log(1.0 + jnp.exp(-jnp.abs(fx)))
        a = log_lb[:, sl]
        log_f = jnp.maximum(a, log_gate) + jnp.log(1.0 + jnp.exp(-jnp.abs(a - log_gate)))
        hi, mid, lo = _split3(log_f)
        b2 = (_dot(tri, hi) + _dot(tri, mid) + _dot(tri, lo)) * LOG2_E
        b_scr[h] = b2
        c_scr[h] = b2 - (log_gate - fx) * LOG2_E
        q_scr[h] = _silu(q_ref[h].astype(F32))

    rows = lax.broadcasted_iota(I32, (SUB, LANES), 0)

    def chunk(ci, carry):
        r0 = pl.multiple_of(ci * C, C)
        for h in range(HG_HEADS):
            sl = slice(h * LANES, (h + 1) * LANES)
            b = b_scr[h, pl.ds(r0, C), :]
            c = c_scr[h, pl.ds(r0, C), :]
            q = q_scr[h, pl.ds(r0, C), :]
            v = i_ref[h, pl.ds(r0, C), :].astype(F32)
            b_last = b[C - 1:C, :]
            q_dec = q * jnp.exp2(b)
            k_st = jnp.exp2(b_last - c)
            st = st_scr[h]
            inter = _dot_nt(q_dec.astype(BF16), st.astype(BF16))
            st_scr[h] = st * jnp.exp2(b_last) + _dot_tn(v.astype(BF16), k_st.astype(BF16))
            acc = [jnp.zeros((SUB, LANES), F32) for _ in range(C // SUB)]
            for s in range(C):
                c_s, v_s = c[s:s + 1, :], v[s:s + 1, :]
                for half in range(s // SUB, C // SUB):
                    rs = slice(half * SUB, (half + 1) * SUB)
                    diff = b[rs, :] - c_s
                    if s > half * SUB:
                        diff = jnp.where(rows >= s - half * SUB, diff, -1e30)
                    sc = jnp.sum(q[rs, :] * jnp.exp2(diff), axis=-1, keepdims=True)
                    acc[half] = acc[half] + sc * v_s
            acc = jnp.concatenate(acc, axis=0) + inter
            o = acc * lax.rsqrt(jnp.mean(acc * acc, axis=-1, keepdims=True) + EPS) * ng[:, sl]
            gg = g_ref[h, pl.ds(r0, C), :].astype(F32)
            o_ref[pl.ds(r0, C), sl] = (o * _silu(gg)).astype(BF16)
        return carry

    lax.fori_loop(0, ts // C, chunk, 0)


def _hgrn(p3, lb, ng, B, S):
    T = B * S
    ts = 256
    nt = S // ts

    def spec(g0):
        return pl.BlockSpec((HG_HEADS, ts, LANES), lambda b, i: (g0 // HG_HEADS, b * nt + i, 0))

    return pl.pallas_call(
        functools.partial(_hgrn_kernel, ts=ts),
        grid=(B, nt),
        in_specs=[spec(G_HQ), spec(G_HF), spec(G_HI), spec(G_HG),
                  pl.BlockSpec((1, HG_HEADS * HG_DK), lambda b, i: (0, 0)),
                  pl.BlockSpec((1, HG_HEADS * HG_DK), lambda b, i: (0, 0))],
        out_specs=pl.BlockSpec((ts, BRANCH_W), lambda b, i: (b * nt + i, 0)),
        out_shape=jax.ShapeDtypeStruct((T, BRANCH_W), BF16),
        scratch_shapes=[pltpu.VMEM((HG_HEADS, ts, LANES), F32),
                        pltpu.VMEM((HG_HEADS, ts, LANES), F32),
                        pltpu.VMEM((HG_HEADS, ts, LANES), F32),
                        pltpu.VMEM((HG_HEADS, HG_DK, HG_DK), F32)],
        compiler_params=_params("parallel", "arbitrary"),
        name="hgrn2",
    )(p3, p3, p3, p3, lb, ng)


def _conv_kernel(a_ref, g_ref, w_ref, cb_ref, lg_ref, lbias_ref, o_ref, z_scr, y_scr, p_scr, *, ts):
    H = CV_HALO

    @pl.when(pl.program_id(1) == 0)
    def _():
        z_scr[0:H, :] = jnp.zeros((H, CV_W), F32)
        z_scr[H + ts:H + ts + SUB, :] = jnp.zeros((SUB, CV_W), F32)

    @pl.when(pl.program_id(1) > 0)
    def _():
        z_scr[0:H, :] = z_scr[ts:ts + H, :]

    for q in range(CV_W // LANES):
        sl = slice(q * LANES, (q + 1) * LANES)
        z_scr[H:H + ts, sl] = a_ref[q].astype(F32) * jax.nn.sigmoid(g_ref[q].astype(F32))

    off = H - (CV_K - 1)
    for q in range(CV_W // LANES):
        sl = slice(q * LANES, (q + 1) * LANES)
        acc = None
        for r in range(SUB):
            part = None
            for a in range((off + CV_K - 1) // SUB + 1):
                j = SUB * a + r - off
                if 0 <= j < CV_K:
                    term = w_ref[j:j + 1, sl] * z_scr[SUB * a:SUB * a + ts + SUB, sl]
                    part = term if part is None else part + term
            p_scr[r] = part
            shifted = p_scr[r, r:r + ts, :]
            acc = shifted if acc is None else acc + shifted
        y_scr[:, sl] = acc + cb_ref[:, sl]

    y = y_scr[...]
    mean = jnp.mean(y, axis=-1, keepdims=True)
    yc = y - mean
    var = jnp.mean(yc * yc, axis=-1, keepdims=True)
    yn = yc * lax.rsqrt(var + EPS) * lg_ref[...] + lbias_ref[...]
    o_ref[...] = _silu(yn).astype(BF16)


def _conv(p3, conv_w, conv_b, ln_g, ln_b, B, S):
    T = B * S
    ts = 256
    nt = S // ts
    ng = CV_W // LANES

    def spec(g0):
        return pl.BlockSpec((ng, ts, LANES), lambda b, i: (g0 // ng, b * nt + i, 0))

    vec = pl.BlockSpec((1, CV_W), lambda b, i: (0, 0))
    return pl.pallas_call(
        functools.partial(_conv_kernel, ts=ts),
        grid=(B, nt),
        in_specs=[spec(G_CA), spec(G_CG), pl.BlockSpec((CV_K, CV_W), lambda b, i: (0, 0)), vec, vec, vec],
        out_specs=pl.BlockSpec((ts, CV_W), lambda b, i: (b * nt + i, 0)),
        out_shape=jax.ShapeDtypeStruct((T, CV_W), BF16),
        scratch_shapes=[pltpu.VMEM((ts + CV_HALO + SUB, CV_W), F32), pltpu.VMEM((ts, CV_W), F32),
                        pltpu.VMEM((SUB, ts + SUB, LANES), F32)],
        compiler_params=_params("parallel", "arbitrary"),
        name="conformer_conv",
    )(p3, p3, conv_w, conv_b, ln_g, ln_b)


def _sb_kernel(q_ref, k_ref, v_ref, qg_ref, kg_ref, o_ref, kn_scr, *, S, tq, nf, n_sub):
    step = pl.program_id(2)
    lane = lax.broadcasted_iota(I32, (1, LANES), 1)
    half0 = lane < SB_DH

    def head_rms(x, g):
        x2 = x * x
        s0 = jnp.sum(jnp.where(half0, x2, 0.0), axis=-1, keepdims=True)
        s1 = jnp.sum(jnp.where(half0, 0.0, x2), axis=-1, keepdims=True)
        ms = jnp.where(half0, s0, s1) * (1.0 / SB_DH)
        return x * lax.rsqrt(ms + EPS) * g

    kc = 512

    @pl.when(step == 0)
    def _():
        def body(c, carry):
            r0 = pl.multiple_of(c * kc, kc)
            kk = k_ref[0, pl.ds(r0, kc), :].astype(F32)
            kn_scr[pl.ds(r0, kc), :] = head_rms(kk, kg_ref[...]).astype(BF16)
            return carry
        lax.fori_loop(0, S // kc, body, 0)

    for u in range(n_sub):
        _sb_tile(step * n_sub + u, q_ref, v_ref, qg_ref, o_ref, kn_scr, slice(u * tq, (u + 1) * tq),
                 head_rms, half0, tq, nf)


def _sb_tile(i, q_ref, v_ref, qg_ref, o_ref, kn_scr, q_rows, head_rms, half0, tq, nf):
    qn = head_rms(q_ref[0, q_rows, :].astype(F32), qg_ref[...]) * (SB_DH ** -0.5 * LOG2_E)
    q2 = jnp.concatenate([jnp.where(half0, qn, 0.0), jnp.where(half0, 0.0, qn)], axis=0).astype(BF16)
    t_idx = lax.broadcasted_iota(I32, (tq, tq), 0)
    s_idx = lax.broadcasted_iota(I32, (tq, tq), 1)
    upper = jnp.where(t_idx > s_idx, 1.0, 0.0).astype(BF16)
    t2 = lax.broadcasted_iota(I32, (2 * tq, tq), 0)
    s2 = lax.broadcasted_iota(I32, (2 * tq, tq), 1)
    past = s2 < jnp.where(t2 >= tq, t2 - tq, t2)

    def blocks(j_first, n, diag, o, r):
        rows = [pl.multiple_of((j_first - m) * tq, tq) for m in range(n)]
        zs = [_dot_nt(q2, kn_scr[pl.ds(r0, tq), :]) for r0 in rows]
        ls = [jnp.log2(1.0 + jnp.exp2(-jnp.abs(z))) for z in zs]
        sps = [jnp.maximum(z, 0.0) + l for z, l in zip(zs, ls)]
        if diag:
            sps[0] = jnp.where(past, sps[0], 0.0)
        bts = [_dot(sp.astype(BF16), upper) for sp in sps]
        pv = None
        for m in range(n):
            w = jnp.exp2(jnp.minimum(zs[m], 0.0) - ls[m] - bts[m] - r)
            if diag and m == 0:
                w = jnp.where(past, w, 0.0)
            t = _dot(w.astype(BF16), v_ref[0, pl.ds(rows[m], tq), :])
            pv = t if pv is None else pv + t
            r = r + jnp.sum(sps[m], axis=-1, keepdims=True)
        return o + jnp.where(half0, pv[:tq, :], pv[tq:, :]), r

    def more(c):
        j, _, r = c
        return jnp.logical_and(j >= 0, jnp.min(r) < SB_DONE_AT * LOG2_E)

    def step(c):
        j, o, r = c
        return (j - 1,) + blocks(j, 1, False, o, r)

    zero = (jnp.zeros((tq, LANES), F32), jnp.zeros((2 * tq, 1), F32))

    @pl.when(i < nf - 1)
    def _():
        out = lax.while_loop(more, step, (i - 1,) + blocks(i, 1, True, *zero))
        o_ref[q_rows, :] = out[1].astype(BF16)

    @pl.when(i >= nf - 1)
    def _():
        out = lax.while_loop(more, step, (i - nf,) + blocks(i, nf, True, *zero))
        o_ref[q_rows, :] = out[1].astype(BF16)


def _stickbreak(p3, qg, kg, B, S):
    T = B * S
    tq, nf, n_sub = 256, 2, 4
    nq = S // (tq * n_sub)
    npair = BRANCH_W // LANES
    vec = pl.BlockSpec((1, LANES), lambda b, p, i: (0, 0))
    return pl.pallas_call(
        functools.partial(_sb_kernel, S=S, tq=tq, nf=nf, n_sub=n_sub),
        grid=(B, npair, nq),
        in_specs=[pl.BlockSpec((1, tq * n_sub, LANES), lambda b, p, i: (G_SQ + p, b * nq + i, 0)),
                  pl.BlockSpec((1, S, LANES), lambda b, p, i: (G_SK + p, b, 0)),
                  pl.BlockSpec((1, S, LANES), lambda b, p, i: (G_SV + p, b, 0)),
                  vec, vec],
        out_specs=pl.BlockSpec((tq * n_sub, LANES), lambda b, p, i: (b * nq + i, p)),
        out_shape=jax.ShapeDtypeStruct((T, BRANCH_W), BF16),
        scratch_shapes=[pltpu.VMEM((S, LANES), BF16)],
        compiler_params=_params("parallel", "parallel", "arbitrary"),
        name="stickbreak_attn",
    )(p3, p3, p3, qg, kg)


def _ret_kernel(q_ref, k_ref, v_ref, g_ref, ng_ref, o_ref, st_scr, *, ts):
    C = RT_CHUNK

    @pl.when(pl.program_id(1) == 0)
    def _():
        st_scr[...] = jnp.zeros_like(st_scr)

    rel = (lax.broadcasted_iota(I32, (C, C), 0) - lax.broadcasted_iota(I32, (C, C), 1)).astype(F32)
    pos = lax.broadcasted_iota(I32, (C, LANES), 0).astype(F32)
    ng = ng_ref[...]
    scale = RT_DH ** -0.5
    for h in range(RT_HEADS):
        sl = slice(h * LANES, (h + 1) * LANES)
        lg = math.log1p(-(2.0 ** (-5 - h)))
        dm = jnp.where(rel >= 0, jnp.exp(lg * jnp.maximum(rel, 0.0)), 0.0) * scale
        qd = jnp.exp(lg * (pos + 1.0))
        kd = jnp.exp(lg * (C - 1.0 - pos)) * scale
        cd = math.exp(lg * C)
        for c in range(ts // C):
            rs = slice(c * C, (c + 1) * C)
            q = q_ref[h, rs, :]
            k = k_ref[h, rs, :]
            v = v_ref[h, rs, :]
            intra = _dot((_dot_nt(q, k) * dm).astype(BF16), v)
            st = st_scr[h]
            inter = _dot((q.astype(F32) * qd).astype(BF16), st.astype(BF16))
            st_scr[h] = cd * st + _dot_tn((k.astype(F32) * kd).astype(BF16), v)
            o = intra + inter
            o = o * lax.rsqrt(jnp.mean(o * o, axis=-1, keepdims=True) + EPS) * ng[:, sl]
            gg = g_ref[h, rs, :].astype(F32)
            o_ref[rs, sl] = (o * _silu(gg)).astype(BF16)


def _retention(p3, ng, B, S):
    T = B * S
    ts = 512
    nt = S // ts

    def spec(g0):
        return pl.BlockSpec((RT_HEADS, ts, LANES), lambda b, i: (g0 // RT_HEADS, b * nt + i, 0))

    return pl.pallas_call(
        functools.partial(_ret_kernel, ts=ts),
        grid=(B, nt),
        in_specs=[spec(G_RQ), spec(G_RK), spec(G_RV), spec(G_RG),
                  pl.BlockSpec((1, RT_HEADS * RT_DH), lambda b, i: (0, 0))],
        out_specs=pl.BlockSpec((ts, BRANCH_W), lambda b, i: (b * nt + i, 0)),
        out_shape=jax.ShapeDtypeStruct((T, BRANCH_W), BF16),
        scratch_shapes=[pltpu.VMEM((RT_HEADS, RT_DH, RT_DH), F32)],
        compiler_params=_params("parallel", "arbitrary"),
        name="retention",
    )(p3, p3, p3, p3, ng)


def _merge_kernel(x_ref, o0, o1, o2, o3, ng_ref, sc_ref, sh_ref, ga_ref, wg_ref, bg_ref, wb_ref, wo_ref, o_ref):
    D = D_MODEL
    x = x_ref[...]
    hb = _rms_mod(x, ng_ref[...], sc_ref[0], sh_ref[0]).astype(BF16)
    merged = None
    for n, oref in enumerate((o0, o1, o2, o3)):
        gate = jax.nn.sigmoid(_dot(hb, wg_ref[0, :, n * D:(n + 1) * D]) + bg_ref[:, n * D:(n + 1) * D])
        y = _dot(oref[...], wb_ref[0, n * BRANCH_W:(n + 1) * BRANCH_W, :])
        merged = gate * y if merged is None else merged + gate * y
    out = _dot(merged.astype(BF16), wo_ref[0])
    o_ref[...] = x + ga_ref[0] * out


def _merge(x2, outs, ng, sc, sh, ga, w_gate_b, b_gate, w_branch_b, w_out_b, l, S):
    T, D = x2.shape
    tm = 512
    tpb = S // tm
    mod = pl.BlockSpec((1, 1, D), lambda i: (i // tpb, 0, 0))
    osp = pl.BlockSpec((tm, BRANCH_W), lambda i: (i, 0))
    once = dict(pipeline_mode=pl.Buffered(1))
    return pl.pallas_call(
        _merge_kernel,
        grid=(T // tm,),
        in_specs=[pl.BlockSpec((tm, D), lambda i: (i, 0)), osp, osp, osp, osp,
                  pl.BlockSpec((1, D), lambda i: (0, 0)), mod, mod, mod,
                  pl.BlockSpec((1, D, N_BRANCH * D), lambda i: (l, 0, 0), **once),
                  pl.BlockSpec((1, N_BRANCH * D), lambda i: (0, 0)),
                  pl.BlockSpec((1, N_BRANCH * BRANCH_W, D), lambda i: (l, 0, 0), **once),
                  pl.BlockSpec((1, D, D), lambda i: (l, 0, 0), **once)],
        out_specs=pl.BlockSpec((tm, D), lambda i: (i, 0)),
        out_shape=jax.ShapeDtypeStruct((T, D), F32),
        compiler_params=_params("parallel"),
        name="branch_merge",
    )(x2, *outs, ng, sc, sh, ga, w_gate_b, b_gate, w_branch_b, w_out_b)


def _route_kernel(x_ref, ng_ref, sc_ref, sh_ref, w_ref, b_ref, h_ref, info_ref, tcnt_ref, tbase_ref, cnt_ref,
                  carry_scr, *, tm):
    @pl.when(pl.program_id(0) == 0)
    def _():
        carry_scr[...] = jnp.zeros_like(carry_scr)

    h = _rms_mod(x_ref[...], ng_ref[...], sc_ref[0], sh_ref[0])
    h_ref[...] = h
    w = w_ref[...]
    h_hi, h_mid, _ = _split3(h)
    w_hi, w_mid, _ = _split3(w)
    lg = _dot(h_hi, w_hi) + _dot(h_mid, w_hi) + _dot(h_hi, w_mid) + b_ref[...]

    lane = lax.broadcasted_iota(I32, (tm, LANES), 1)
    lane_f = lane.astype(F32)
    neg = -jnp.inf
    big = float(LANES)

    def first_argmax(vals):
        m = jnp.max(vals, axis=-1, keepdims=True)
        idx = jnp.min(jnp.where(vals == m, lane_f, big), axis=-1, keepdims=True)
        return m, idx

    gmask = lane < N_GROUPS
    gl = jnp.where(gmask, lg, neg)
    gmax, gidx = first_argmax(gl)
    gsum = jnp.sum(jnp.where(gmask, jnp.exp(lg - gmax), 0.0), axis=-1, keepdims=True)
    g_p = 1.0 / gsum
    lo = N_GROUPS + EXP_PER_GROUP * gidx
    emask = (lane_f >= lo) & (lane_f < lo + EXP_PER_GROUP)
    el = jnp.where(emask, lg, neg)
    v1, i1 = first_argmax(el)
    el2 = jnp.where(lane_f == i1, neg, el)
    v2, i2 = first_argmax(el2)
    e2x = jnp.exp(v2 - v1)
    p1 = 1.0 / (1.0 + e2x)
    p2 = e2x * p1
    e1 = i1 - N_GROUPS
    e2 = i2 - N_GROUPS

    onehot = jnp.where((lane_f == e1) | (lane_f == e2), 1.0, 0.0)
    rr = lax.broadcasted_iota(I32, (tm, tm), 0)
    cc = lax.broadcasted_iota(I32, (tm, tm), 1)
    before = jnp.where(cc < rr, 1.0, 0.0).astype(BF16)
    local_rank = _dot(before, onehot.astype(BF16))
    tile_cnt = jnp.sum(onehot, axis=0, keepdims=True)
    run_len = jnp.floor((tile_cnt + (DMA_ROWS - 1)) * (1.0 / DMA_ROWS)) * DMA_ROWS
    er = lax.broadcasted_iota(I32, (LANES, LANES), 0)
    ec = lax.broadcasted_iota(I32, (LANES, LANES), 1)
    lower_e = jnp.where(er < ec, 1.0, 0.0).astype(BF16)
    run_start = _dot(jnp.broadcast_to(run_len, (SUB, LANES)).astype(BF16), lower_e)[0:1, :]
    slot = run_start + local_rank
    pos1 = jnp.sum(jnp.where(lane_f == e1, slot, 0.0), axis=-1, keepdims=True)
    pos2 = jnp.sum(jnp.where(lane_f == e2, slot, 0.0), axis=-1, keepdims=True)

    base = carry_scr[0:1, :]
    total = base + run_len
    carry_scr[...] = jnp.broadcast_to(total, carry_scr.shape)
    cnt_ref[...] = jnp.broadcast_to(total, cnt_ref.shape)
    tcnt_ref[0] = jnp.broadcast_to(run_len * (1.0 / DMA_ROWS), (SUB, LANES))
    tbase_ref[0] = jnp.broadcast_to(base, (SUB, LANES))

    info = jnp.where(lane == 0, e1, 0.0)
    info = jnp.where(lane == 1, e2, info)
    info = jnp.where(lane == 2, g_p * p1, info)
    info = jnp.where(lane == 3, g_p * p2, info)
    info = jnp.where(lane == 4, pos1, info)
    info = jnp.where(lane == 5, pos2, info)
    info_ref[...] = info


def _route(x2, ng, sc, sh, w_r, b_r, S):
    T, D = x2.shape
    tm = MOE_TILE
    tpb = S // tm
    nt = T // tm
    mod = pl.BlockSpec((1, 1, D), lambda i: (i // tpb, 0, 0))
    stat = pl.BlockSpec((1, SUB, LANES), lambda i: (i, 0, 0))
    return pl.pallas_call(
        functools.partial(_route_kernel, tm=tm),
        grid=(nt,),
        in_specs=[pl.BlockSpec((tm, D), lambda i: (i, 0)),
                  pl.BlockSpec((1, D), lambda i: (0, 0)), mod, mod,
                  pl.BlockSpec((D, LANES), lambda i: (0, 0)),
                  pl.BlockSpec((1, LANES), lambda i: (0, 0))],
        out_specs=[pl.BlockSpec((tm, D), lambda i: (i, 0)),
                   pl.BlockSpec((tm, LANES), lambda i: (i, 0)),
                   stat, stat,
                   pl.BlockSpec((SUB, LANES), lambda i: (0, 0))],
        out_shape=[jax.ShapeDtypeStruct((T, D), F32),
                   jax.ShapeDtypeStruct((T, LANES), F32),
                   jax.ShapeDtypeStruct((nt, SUB, LANES), F32),
                   jax.ShapeDtypeStruct((nt, SUB, LANES), F32),
                   jax.ShapeDtypeStruct((SUB, LANES), F32)],
        scratch_shapes=[pltpu.VMEM((SUB, LANES), F32)],
        compiler_params=_params("arbitrary"),
        name="moe_route",
    )(x2, ng, sc, sh, w_r, b_r)


def _start_runs(nch_ref, row_ref, tile, make_copy):
    def per_expert(e, issued):
        nch = nch_ref[tile * N_EXPERTS + e]
        row = row_ref[tile * N_EXPERTS + e]

        def issue(c, z):
            make_copy(pl.multiple_of((issued + c) * DMA_ROWS, DMA_ROWS),
                      pl.multiple_of(row + c * DMA_ROWS, DMA_ROWS)).start()
            return z

        lax.fori_loop(0, nch, issue, 0)
        return issued + nch

    return lax.fori_loop(0, N_EXPERTS, per_expert, jnp.int32(0))


def _wait_runs(count, make_copy):
    lax.fori_loop(0, count, lambda c, z: (make_copy(0, 0).wait(), z)[1], 0)


def _dispatch_kernel(nch_ref, row_ref, tail_n_ref, tail_row_ref, used_ref, h_ref, info_ref, xb_ref, srt, zero, sems,
                     pending, *, tm, n_slot):
    i = pl.program_id(0)
    cur = lax.rem(i, 2)
    sem = sems.at[0]

    def copy_zero(row):
        return pltpu.make_async_copy(zero.at[pl.ds(0, DMA_ROWS), :],
                                     xb_ref.at[pl.ds(pl.multiple_of(row, DMA_ROWS), DMA_ROWS), :], sem)

    def copy_zero_block(blk):
        return pltpu.make_async_copy(zero, xb_ref.at[pl.ds(pl.multiple_of(blk * MOE_ROWS, MOE_ROWS), MOE_ROWS), :], sem)

    @pl.when(i == 0)
    def _():
        zero[...] = jnp.zeros_like(zero)

        def per_expert(e, issued):
            lax.fori_loop(0, tail_n_ref[e], lambda c, z: (copy_zero(tail_row_ref[e] + c * DMA_ROWS).start(), z)[1], 0)
            return issued + tail_n_ref[e]

        issued = lax.fori_loop(0, N_EXPERTS, per_expert, jnp.int32(0))
        lax.fori_loop(0, issued, lambda c, z: (copy_zero(0).wait(), z)[1], 0)
        n_blocks = xb_ref.shape[0] // MOE_ROWS
        lax.fori_loop(used_ref[0], n_blocks, lambda b, z: (copy_zero_block(b).start(), z)[1], 0)
        lax.fori_loop(used_ref[0], n_blocks, lambda b, z: (copy_zero_block(0).wait(), z)[1], 0)

    info_t = info_ref[...].T
    slot = lax.broadcasted_iota(I32, (n_slot, tm), 0).astype(F32)
    perm = jnp.where((slot == info_t[4:5, :]) | (slot == info_t[5:6, :]), 1.0, 0.0).astype(BF16)
    srt[cur] = _dot(perm, h_ref[...].astype(BF16))

    def copier(buf):
        def copy(slot_row, buf_row):
            return pltpu.make_async_copy(srt.at[buf, pl.ds(slot_row, DMA_ROWS), :],
                                         xb_ref.at[pl.ds(buf_row, DMA_ROWS), :], sems.at[buf])
        return copy

    @pl.when(i > 0)
    def _():
        _wait_runs(pending[0], copier(1 - cur))

    pending[0] = _start_runs(nch_ref, row_ref, i, copier(cur))

    @pl.when(i == pl.num_programs(0) - 1)
    def _():
        _wait_runs(pending[0], copier(cur))


def _dispatch(h2, info, nch_tab, row_tab, tail_n, tail_row, n_used, n_rows):
    T, D = h2.shape
    tm = MOE_TILE
    n_slot = 2 * tm + N_EXPERTS * DMA_ROWS
    grid_spec = pltpu.PrefetchScalarGridSpec(
        num_scalar_prefetch=5,
        grid=(T // tm,),
        in_specs=[pl.BlockSpec((tm, D), lambda i, *_: (i, 0)),
                  pl.BlockSpec((tm, LANES), lambda i, *_: (i, 0))],
        out_specs=pl.BlockSpec(memory_space=pl.ANY),
        scratch_shapes=[pltpu.VMEM((2, n_slot, D), F32), pltpu.VMEM((MOE_ROWS, D), F32),
                        pltpu.SemaphoreType.DMA((2,)), pltpu.SMEM((1,), I32)],
    )
    return pl.pallas_call(
        functools.partial(_dispatch_kernel, tm=tm, n_slot=n_slot),
        grid_spec=grid_spec,
        out_shape=jax.ShapeDtypeStruct((n_rows, D), F32),
        compiler_params=_params("arbitrary"),
        name="moe_dispatch",
    )(nch_tab, row_tab, tail_n, tail_row, n_used, h2, info)


def _expert_kernel(be_ref, used_ref, x_ref, w1_ref, w3_ref, w2_ref, y_ref, w1_b, w3_b, w2_b):
    i = pl.program_id(0)
    live = i < used_ref[0]

    @pl.when(live & ((i == 0) | (be_ref[i] != be_ref[jnp.maximum(i - 1, 0)])))
    def _():
        w1_b[...] = w1_ref[0, 0].astype(BF16)
        w3_b[...] = w3_ref[0, 0].astype(BF16)
        w2_b[...] = w2_ref[0, 0].astype(BF16)

    @pl.when(live)
    def _():
        x = x_ref[...].astype(BF16)
        a = _silu(_dot(x, w1_b[...])) * _dot(x, w3_b[...])
        y_ref[...] = _dot(a.astype(BF16), w2_b[...])

    @pl.when(jnp.logical_not(live))
    def _():
        y_ref[...] = jnp.zeros_like(y_ref)


def _experts(xb, blk_e, n_used, w1, w3, w2, l):
    P, D = xb.shape
    R = MOE_ROWS
    grid_spec = pltpu.PrefetchScalarGridSpec(
        num_scalar_prefetch=2,
        grid=(P // R,),
        in_specs=[pl.BlockSpec((R, D), lambda i, be, nu: (i, 0)),
                  pl.BlockSpec((1, 1, D, D_EXPERT), lambda i, be, nu: (l, be[i], 0, 0)),
                  pl.BlockSpec((1, 1, D, D_EXPERT), lambda i, be, nu: (l, be[i], 0, 0)),
                  pl.BlockSpec((1, 1, D_EXPERT, D), lambda i, be, nu: (l, be[i], 0, 0))],
        out_specs=pl.BlockSpec((R, D), lambda i, be, nu: (i, 0)),
        scratch_shapes=[pltpu.VMEM((D, D_EXPERT), BF16), pltpu.VMEM((D, D_EXPERT), BF16),
                        pltpu.VMEM((D_EXPERT, D), BF16)],
    )
    return pl.pallas_call(
        _expert_kernel,
        grid_spec=grid_spec,
        out_shape=jax.ShapeDtypeStruct((P, D), F32),
        compiler_params=_params("arbitrary"),
        name="moe_experts",
    )(blk_e, n_used, xb, w1, w3, w2)


def _combine_kernel(nch_ref, row_ref, x_ref, info_ref, ga_ref, yb_ref, o_ref, srt, sems, pending, *, tm, n_slot):
    i = pl.program_id(0)
    cur = lax.rem(i, 2)

    def copier(buf):
        def copy(slot_row, buf_row):
            return pltpu.make_async_copy(yb_ref.at[pl.ds(buf_row, DMA_ROWS), :],
                                         srt.at[buf, pl.ds(slot_row, DMA_ROWS), :], sems.at[buf])
        return copy

    @pl.when(i == 0)
    def _():
        pending[0] = _start_runs(nch_ref, row_ref, 0, copier(0))

    @pl.when(i + 1 < pl.num_programs(0))
    def _():
        pending[1 - cur] = _start_runs(nch_ref, row_ref, i + 1, copier(1 - cur))

    covered = pending[cur]
    _wait_runs(covered, copier(cur))

    def clear(r, z):
        srt[cur, pl.ds(pl.multiple_of(r * DMA_ROWS, DMA_ROWS), DMA_ROWS), :] = jnp.zeros((DMA_ROWS, srt.shape[2]), F32)
        return z

    lax.fori_loop(covered, n_slot // DMA_ROWS, clear, 0)

    info = info_ref[...]
    lane = lax.broadcasted_iota(I32, (tm, LANES), 1)

    def col(k):
        return jnp.sum(jnp.where(lane == k, info, 0.0), axis=-1, keepdims=True)

    slot = lax.broadcasted_iota(I32, (tm, n_slot), 1).astype(F32)
    gates = jnp.where(slot == col(4), col(2), 0.0) + jnp.where(slot == col(5), col(3), 0.0)
    mixed = _dot(gates.astype(BF16), srt[cur].astype(BF16))
    o_ref[...] = x_ref[...] + ga_ref[0] * mixed


def _combine(x2, info, nch_tab, row_tab, ga, yb, S):
    T, D = x2.shape
    tm = MOE_TILE
    tpb = S // tm
    n_slot = 2 * tm + N_EXPERTS * DMA_ROWS
    grid_spec = pltpu.PrefetchScalarGridSpec(
        num_scalar_prefetch=2,
        grid=(T // tm,),
        in_specs=[pl.BlockSpec((tm, D), lambda i, c, d: (i, 0)),
                  pl.BlockSpec((tm, LANES), lambda i, c, d: (i, 0)),
                  pl.BlockSpec((1, 1, D), lambda i, c, d: (i // tpb, 0, 0)),
                  pl.BlockSpec(memory_space=pl.ANY)],
        out_specs=pl.BlockSpec((tm, D), lambda i, c, d: (i, 0)),
        scratch_shapes=[pltpu.VMEM((2, n_slot, D), F32), pltpu.SemaphoreType.DMA((2,)), pltpu.SMEM((2,), I32)],
    )
    return pl.pallas_call(
        functools.partial(_combine_kernel, tm=tm, n_slot=n_slot),
        grid_spec=grid_spec,
        out_shape=jax.ShapeDtypeStruct((T, D), F32),
        compiler_params=_params("arbitrary"),
        name="moe_combine",
    )(nch_tab, row_tab, x2, info, ga, yb)


def _moe(x2, ng, sc, sh, ga, w_r, b_r, w1, w3, w2, l, S):
    T, D = x2.shape
    R = MOE_ROWS
    nt = T // MOE_TILE
    n_rows = (T * 2 + N_EXPERTS * (nt * (DMA_ROWS - 1) + R - 1) + R - 1) // R * R
    h2, info, tnch, tbase, tot = _route(x2, ng, sc, sh, w_r, b_r, S)
    owned = tot[0, :N_EXPERTS].astype(I32)
    padded = (owned + R - 1) // R * R
    pad_end = jnp.cumsum(padded)
    pad_start = pad_end - padded
    nch_tab = tnch[:, 0, :N_EXPERTS].astype(I32).reshape(-1)
    row_tab = (tbase[:, 0, :N_EXPERTS].astype(I32) + pad_start[None, :]).reshape(-1)
    tail_n = (padded - owned) // DMA_ROWS
    tail_row = pad_start + owned
    blk_start = jnp.arange(n_rows // R, dtype=I32) * R
    blk_e = jnp.minimum(jnp.sum((pad_end[None, :] <= blk_start[:, None]).astype(I32), axis=1), N_EXPERTS - 1)
    n_used = (pad_end[-1:] // R).astype(I32)
    xb = _dispatch(h2, info, nch_tab, row_tab, tail_n, tail_row, n_used, n_rows)
    yb = _experts(xb, blk_e, n_used, w1, w3, w2, l)
    return _combine(x2, info, nch_tab, row_tab, ga, yb, S)


def kernel(x, c, ada_w, ada_b, norm1_g, norm2_g, w_in, hgrn_lb, hgrn_norm_g, conv_w, conv_b, conv_ln_g,
           conv_ln_b, sb_qnorm_g, sb_knorm_g, ret_norm_g, w_branch, w_gate, b_gate, w_out, router_group_w,
           router_group_b, router_expert_w, router_expert_b, expert_w1, expert_w3, expert_w2):
    B, S, D = x.shape
    L = ada_w.shape[0]
    T = B * S
    sm = jax.nn.softmax(hgrn_lb.astype(F32), axis=0)
    lower_bounds = jnp.cumsum(sm, axis=0) - sm[0]
    mods = _mods(c, ada_w, ada_b).reshape(L, B, 6, 1, D)
    w_in_b = w_in.astype(BF16)
    w_gate_b = w_gate.astype(BF16)
    w_branch_b = w_branch.astype(BF16)
    w_out_b = w_out.astype(BF16)
    n_r = N_GROUPS + N_EXPERTS
    w_r = jnp.zeros((L, D, LANES), F32).at[:, :, :N_GROUPS].set(router_group_w).at[:, :, N_GROUPS:n_r].set(router_expert_w)
    b_r = jnp.zeros((L, 1, LANES), F32).at[:, 0, :N_GROUPS].set(router_group_b).at[:, 0, N_GROUPS:n_r].set(router_expert_b)

    x2 = x.reshape(T, D)
    for l in range(L):
        sh1, sc1, g1, sh2, sc2, g2 = (mods[l, :, k] for k in range(6))
        n1 = norm1_g[l][None, :]
        p3 = _in_proj(x2, n1, sc1, sh1, w_in_b, l, S)
        o_hg = _hgrn(p3, lower_bounds[l][None, :], hgrn_norm_g[l][None, :], B, S)
        o_cv = _conv(p3, conv_w[l], conv_b[l][None, :], conv_ln_g[l][None, :], conv_ln_b[l][None, :], B, S)
        qg = jnp.tile(sb_qnorm_g[l], 2)[None, :]
        kg = jnp.tile(sb_knorm_g[l], 2)[None, :]
        o_sb = _stickbreak(p3, qg, kg, B, S)
        o_rt = _retention(p3, ret_norm_g[l][None, :], B, S)
        x2 = _merge(x2, (o_hg, o_cv, o_sb, o_rt), n1, sc1, sh1, g1, w_gate_b, b_gate[l][None, :],
                    w_branch_b, w_out_b, l, S)
        x2 = _moe(x2, norm2_g[l][None, :], sc2, sh2, g2, w_r[l], b_r[l], expert_w1, expert_w3, expert_w2, l, S)
    return x2.reshape(B, S, D)
```

```python
import functools
import math

import jax
import jax.numpy as jnp
from jax import lax
from jax.experimental import pallas as pl
from jax.experimental.pallas import tpu as pltpu

F32 = jnp.float32
BF16 = jnp.bfloat16
I32 = jnp.int32

LANES = 128
SUB = 8
VMEM_LIMIT = 56 * 1024 * 1024

D_MODEL = 1024
EPS = 1e-6
LOG_TINY = -87.0
LOG2_E = 1.4426950408889634
HG_HEADS, HG_DK, HG_CHUNK = 4, 128, 16
CV_W, CV_K = 512, 31
CV_HALO = 32
SB_DH = 64
RT_HEADS, RT_DH, RT_CHUNK = 4, 128, 128
N_BRANCH = 4
BRANCH_W = 512
N_GROUPS, EXP_PER_GROUP, N_EXPERTS = 4, 8, 32
D_EXPERT = 512
MOE_ROWS = 512
MOE_TILE = 256
DMA_ROWS = 8
SB_DONE_AT = 104.0

G_HQ, G_HF, G_HI, G_HG = 0, 4, 8, 12
G_CA, G_CG = 16, 20
G_SQ, G_SK, G_SV = 24, 28, 32
G_RQ, G_RK, G_RV, G_RG = 36, 40, 44, 48
N_COL_GROUPS = 52


def _dot(a, b):
    return jnp.dot(a, b, preferred_element_type=F32)


def _dot_nt(a, b):
    return lax.dot_general(a, b, (((1,), (1,)), ((), ())), preferred_element_type=F32)


def _dot_tn(a, b):
    return lax.dot_general(a, b, (((0,), (0,)), ((), ())), preferred_element_type=F32)


def _split3(x):
    hi = x.astype(BF16)
    r1 = x - hi.astype(F32)
    mid = r1.astype(BF16)
    lo = (r1 - mid.astype(F32)).astype(BF16)
    return hi, mid, lo


def _rms_mod(x, g, sc, sh):
    ms = jnp.mean(x * x, axis=-1, keepdims=True)
    return x * lax.rsqrt(ms + EPS) * g * (1.0 + sc) + sh


def _silu(x):
    return x * jax.nn.sigmoid(x)


def _params(*sem):
    return pltpu.CompilerParams(dimension_semantics=sem, vmem_limit_bytes=VMEM_LIMIT)


def _mods_kernel(c_ref, w_ref, b_ref, o_ref):
    c = c_ref[...]
    sc = _silu(c)
    w = w_ref[0]
    acc = None
    for cp in _split3(sc):
        for wp in _split3(w)[:2]:
            t = _dot(cp, wp)
            acc = t if acc is None else acc + t
    o_ref[0] = acc + b_ref[0]


def _mods(c, ada_w, ada_b):
    L, D, E = ada_w.shape
    B = c.shape[0]
    rows = 8
    tn = 1536
    cp = jnp.zeros((rows, D), F32).at[:B].set(c)
    out = pl.pallas_call(
        _mods_kernel,
        grid=(L, E // tn),
        in_specs=[pl.BlockSpec((rows, D), lambda l, j: (0, 0)),
                  pl.BlockSpec((1, D, tn), lambda l, j: (l, 0, j)),
                  pl.BlockSpec((1, 1, tn), lambda l, j: (l, 0, j))],
        out_specs=pl.BlockSpec((1, rows, tn), lambda l, j: (l, 0, j)),
        out_shape=jax.ShapeDtypeStruct((L, rows, E), F32),
        compiler_params=_params("parallel", "parallel"),
        name="adaln_mods",
    )(cp, ada_w, ada_b.reshape(L, 1, E))
    return out[:, :B]


def _in_kernel(x_ref, g_ref, sc_ref, sh_ref, w_ref, o_ref, *, tn):
    hb = _rms_mod(x_ref[...], g_ref[...], sc_ref[0], sh_ref[0]).astype(BF16)
    gpt = tn // LANES
    for j in range(w_ref.shape[2] // tn):
        res = _dot(hb, w_ref[0, :, j * tn:(j + 1) * tn])
        for q in range(gpt):
            o_ref[j * gpt + q] = res[:, q * LANES:(q + 1) * LANES].astype(BF16)


def _in_proj(x2, g, sc, sh, w_in_b, l, S):
    T, D = x2.shape
    tm, tn = 512, 512
    cols = w_in_b.shape[2]
    tpb = S // tm
    mod = pl.BlockSpec((1, 1, D), lambda i: (i // tpb, 0, 0))
    return pl.pallas_call(
        functools.partial(_in_kernel, tn=tn),
        grid=(T // tm,),
        in_specs=[pl.BlockSpec((tm, D), lambda i: (i, 0)),
                  pl.BlockSpec((1, D), lambda i: (0, 0)), mod, mod,
                  pl.BlockSpec((1, D, cols), lambda i: (l, 0, 0), pipeline_mode=pl.Buffered(1))],
        out_specs=pl.BlockSpec((cols // LANES, tm, LANES), lambda i: (0, i, 0)),
        out_shape=jax.ShapeDtypeStruct((cols // LANES, T, LANES), BF16),
        compiler_params=_params("parallel"),
        name="in_proj",
    )(x2, g, sc, sh, w_in_b)


def _hgrn_kernel(q_ref, f_ref, i_ref, g_ref, lb_ref, ng_ref, o_ref, b_scr, c_scr, q_scr, u_scr, st_scr, *, ts):
    C = HG_CHUNK

    @pl.when(pl.program_id(1) == 0)
    def _():
        st_scr[...] = jnp.zeros_like(st_scr)

    lb = lb_ref[...]
    log_lb = jnp.maximum(jnp.log(jnp.maximum(lb, 1e-30)), LOG_TINY)
    log_1m = jnp.log1p(-lb)
    ng = ng_ref[...]
    rr = lax.broadcasted_iota(I32, (ts, ts), 0)
    cc = lax.broadcasted_iota(I32, (ts, ts), 1)
    shift = C.bit_length() - 1
    tri = jnp.where((jnp.right_shift(rr, shift) == jnp.right_shift(cc, shift)) & (cc <= rr), 1.0, 0.0).astype(BF16)

    for h in range(HG_HEADS):
        sl = slice(h * LANES, (h + 1) * LANES)
        fx = f_ref[h].astype(F32)
        log_gate = log_1m[:, sl] + jnp.minimum(fx, 0.0) - jnp.log(1.0 + jnp.exp(-jnp.abs(fx)))
        a = log_lb[:, sl]
        log_f = jnp.maximum(a, log_gate) + jnp.log(1.0 + jnp.exp(-jnp.abs(a - log_gate)))
        hi, mid, lo = _split3(log_f)
        b2 = (_dot(tri, hi) + _dot(tri, mid) + _dot(tri, lo)) * LOG2_E
        b_scr[h] = b2
        c_scr[h] = b2 - (log_gate - fx) * LOG2_E
        q_scr[h] = _silu(q_ref[h].astype(F32))

    rows = lax.broadcasted_iota(I32, (SUB, LANES), 0)

    def finish(h, r):
        sl = slice(h * LANES, (h + 1) * LANES)
        u = u_scr[h, pl.ds(r, C), :]
        o = u * lax.rsqrt(jnp.mean(u * u, axis=-1, keepdims=True) + EPS) * ng[:, sl]
        gg = g_ref[h, pl.ds(r, C), :].astype(F32)
        o_ref[pl.ds(r, C), sl] = (o * _silu(gg)).astype(BF16)

    u_scr[:, 0:C, :] = jnp.zeros((HG_HEADS, C, LANES), F32)

    def chunk(ci, carry):
        r0 = pl.multiple_of(ci * C, C)
        r_prev = pl.multiple_of(jnp.maximum(ci - 1, 0) * C, C)
        for h in range(HG_HEADS):
            finish(h, r_prev)
        for h in range(HG_HEADS):
            sl = slice(h * LANES, (h + 1) * LANES)
            b = b_scr[h, pl.ds(r0, C), :]
            c = c_scr[h, pl.ds(r0, C), :]
            q = q_scr[h, pl.ds(r0, C), :]
            v = i_ref[h, pl.ds(r0, C), :].astype(F32)
            b_last = b[C - 1:C, :]
            q_dec = q * jnp.exp2(b)
            k_st = jnp.exp2(b_last - c)
            st = st_scr[h]
            inter = _dot_nt(q_dec.astype(BF16), st.astype(BF16))
            st_scr[h] = st * jnp.exp2(b_last) + _dot_tn(v.astype(BF16), k_st.astype(BF16))
            acc = [jnp.zeros((SUB, LANES), F32) for _ in range(C // SUB)]
            for s in range(C):
                c_s, v_s = c[s:s + 1, :], v[s:s + 1, :]
                for half in range(s // SUB, C // SUB):
                    rs = slice(half * SUB, (half + 1) * SUB)
                    diff = b[rs, :] - c_s
                    if s > half * SUB:
                        diff = jnp.where(rows >= s - half * SUB, diff, -1e30)
                    sc = jnp.sum(q[rs, :] * jnp.exp2(diff), axis=-1, keepdims=True)
                    acc[half] = acc[half] + sc * v_s
            u_scr[h, pl.ds(r0, C), :] = jnp.concatenate(acc, axis=0) + inter
        return carry

    lax.fori_loop(0, ts // C, chunk, 0)
    for h in range(HG_HEADS):
        finish(h, ts - C)


def _hgrn(p3, lb, ng, B, S):
    T = B * S
    ts = 256
    nt = S // ts

    def spec(g0):
        return pl.BlockSpec((HG_HEADS, ts, LANES), lambda b, i: (g0 // HG_HEADS, b * nt + i, 0))

    return pl.pallas_call(
        functools.partial(_hgrn_kernel, ts=ts),
        grid=(B, nt),
        in_specs=[spec(G_HQ), spec(G_HF), spec(G_HI), spec(G_HG),
                  pl.BlockSpec((1, HG_HEADS * HG_DK), lambda b, i: (0, 0)),
                  pl.BlockSpec((1, HG_HEADS * HG_DK), lambda b, i: (0, 0))],
        out_specs=pl.BlockSpec((ts, BRANCH_W), lambda b, i: (b * nt + i, 0)),
        out_shape=jax.ShapeDtypeStruct((T, BRANCH_W), BF16),
        scratch_shapes=[pltpu.VMEM((HG_HEADS, ts, LANES), F32),
                        pltpu.VMEM((HG_HEADS, ts, LANES), F32),
                        pltpu.VMEM((HG_HEADS, ts, LANES), F32),
                        pltpu.VMEM((HG_HEADS, ts, LANES), F32),
                        pltpu.VMEM((HG_HEADS, HG_DK, HG_DK), F32)],
        compiler_params=_params("parallel", "arbitrary"),
        name="hgrn2",
    )(p3, p3, p3, p3, lb, ng)


def _conv_kernel(a_ref, g_ref, w_ref, cb_ref, lg_ref, lbias_ref, o_ref, z_scr, y_scr, p_scr, *, ts):
    H = CV_HALO

    @pl.when(pl.program_id(1) == 0)
    def _():
        z_scr[0:H, :] = jnp.zeros((H, CV_W), F32)
        z_scr[H + ts:H + ts + SUB, :] = jnp.zeros((SUB, CV_W), F32)

    @pl.when(pl.program_id(1) > 0)
    def _():
        z_scr[0:H, :] = z_scr[ts:ts + H, :]

    for q in range(CV_W // LANES):
        sl = slice(q * LANES, (q + 1) * LANES)
        z_scr[H:H + ts, sl] = a_ref[q].astype(F32) * jax.nn.sigmoid(g_ref[q].astype(F32))

    off = H - (CV_K - 1)
    for q in range(CV_W // LANES):
        sl = slice(q * LANES, (q + 1) * LANES)
        acc = None
        for r in range(SUB):
            part = None
            for a in range((off + CV_K - 1) // SUB + 1):
                j = SUB * a + r - off
                if 0 <= j < CV_K:
                    term = w_ref[j:j + 1, sl] * z_scr[SUB * a:SUB * a + ts + SUB, sl]
                    part = term if part is None else part + term
            p_scr[r] = part
            shifted = p_scr[r, r:r + ts, :]
            acc = shifted if acc is None else acc + shifted
        y_scr[:, sl] = acc + cb_ref[:, sl]

    y = y_scr[...]
    mean = jnp.mean(y, axis=-1, keepdims=True)
    yc = y - mean
    var = jnp.mean(yc * yc, axis=-1, keepdims=True)
    yn = yc * lax.rsqrt(var + EPS) * lg_ref[...] + lbias_ref[...]
    o_ref[...] = _silu(yn).astype(BF16)


def _conv(p3, conv_w, conv_b, ln_g, ln_b, B, S):
    T = B * S
    ts = 256
    nt = S // ts
    ng = CV_W // LANES

    def spec(g0):
        return pl.BlockSpec((ng, ts, LANES), lambda b, i: (g0 // ng, b * nt + i, 0))

    vec = pl.BlockSpec((1, CV_W), lambda b, i: (0, 0))
    return pl.pallas_call(
        functools.partial(_conv_kernel, ts=ts),
        grid=(B, nt),
        in_specs=[spec(G_CA), spec(G_CG), pl.BlockSpec((CV_K, CV_W), lambda b, i: (0, 0)), vec, vec, vec],
        out_specs=pl.BlockSpec((ts, CV_W), lambda b, i: (b * nt + i, 0)),
        out_shape=jax.ShapeDtypeStruct((T, CV_W), BF16),
        scratch_shapes=[pltpu.VMEM((ts + CV_HALO + SUB, CV_W), F32), pltpu.VMEM((ts, CV_W), F32),
                        pltpu.VMEM((SUB, ts + SUB, LANES), F32)],
        compiler_params=_params("parallel", "arbitrary"),
        name="conformer_conv",
    )(p3, p3, conv_w, conv_b, ln_g, ln_b)


def _sb_kernel(q_ref, k_ref, v_ref, qg_ref, kg_ref, o_ref, kn_scr, *, S, tq, nf, n_sub):
    step = pl.program_id(2)
    lane = lax.broadcasted_iota(I32, (1, LANES), 1)
    half0 = lane < SB_DH

    def head_rms(x, g):
        x2 = x * x
        s0 = jnp.sum(jnp.where(half0, x2, 0.0), axis=-1, keepdims=True)
        s1 = jnp.sum(jnp.where(half0, 0.0, x2), axis=-1, keepdims=True)
        ms = jnp.where(half0, s0, s1) * (1.0 / SB_DH)
        return x * lax.rsqrt(ms + EPS) * g

    kc = 512

    @pl.when(step == 0)
    def _():
        def body(c, carry):
            r0 = pl.multiple_of(c * kc, kc)
            kk = k_ref[0, pl.ds(r0, kc), :].astype(F32)
            kn_scr[pl.ds(r0, kc), :] = head_rms(kk, kg_ref[...]).astype(BF16)
            return carry
        lax.fori_loop(0, S // kc, body, 0)

    for u in range(n_sub):
        _sb_tile(step * n_sub + u, q_ref, v_ref, qg_ref, o_ref, kn_scr, slice(u * tq, (u + 1) * tq),
                 head_rms, half0, tq, nf)


def _sb_tile(i, q_ref, v_ref, qg_ref, o_ref, kn_scr, q_rows, head_rms, half0, tq, nf):
    qn = head_rms(q_ref[0, q_rows, :].astype(F32), qg_ref[...]) * (SB_DH ** -0.5 * LOG2_E)
    q2 = jnp.concatenate([jnp.where(half0, qn, 0.0), jnp.where(half0, 0.0, qn)], axis=0).astype(BF16)
    t_idx = lax.broadcasted_iota(I32, (tq, tq), 0)
    s_idx = lax.broadcasted_iota(I32, (tq, tq), 1)
    upper = jnp.where(t_idx > s_idx, 1.0, 0.0).astype(BF16)
    t2 = lax.broadcasted_iota(I32, (2 * tq, tq), 0)
    s2 = lax.broadcasted_iota(I32, (2 * tq, tq), 1)
    past = s2 < jnp.where(t2 >= tq, t2 - tq, t2)

    def blocks(j_first, n, diag, o, r):
        rows = [pl.multiple_of((j_first - m) * tq, tq) for m in range(n)]
        zs = [_dot_nt(q2, kn_scr[pl.ds(r0, tq), :]) for r0 in rows]
        ls = [jnp.log2(1.0 + jnp.exp2(-jnp.abs(z))) for z in zs]
        sps = [jnp.maximum(z, 0.0) + l for z, l in zip(zs, ls)]
        if diag:
            sps[0] = jnp.where(past, sps[0], 0.0)
        bts = [_dot(sp.astype(BF16), upper) for sp in sps]
        pv = None
        for m in range(n):
            w = jnp.exp2(jnp.minimum(zs[m], 0.0) - ls[m] - bts[m] - r)
            if diag and m == 0:
                w = jnp.where(past, w, 0.0)
            t = _dot(w.astype(BF16), v_ref[0, pl.ds(rows[m], tq), :])
            pv = t if pv is None else pv + t
            r = r + jnp.sum(sps[m], axis=-1, keepdims=True)
        return o + jnp.where(half0, pv[:tq, :], pv[tq:, :]), r

    def more(c):
        j, _, r = c
        return jnp.logical_and(j >= 0, jnp.min(r) < SB_DONE_AT * LOG2_E)

    def step(c):
        j, o, r = c
        return (j - 1,) + blocks(j, 1, False, o, r)

    zero = (jnp.zeros((tq, LANES), F32), jnp.zeros((2 * tq, 1), F32))

    @pl.when(i < nf - 1)
    def _():
        out = lax.while_loop(more, step, (i - 1,) + blocks(i, 1, True, *zero))
        o_ref[q_rows, :] = out[1].astype(BF16)

    @pl.when(i >= nf - 1)
    def _():
        out = lax.while_loop(more, step, (i - nf,) + blocks(i, nf, True, *zero))
        o_ref[q_rows, :] = out[1].astype(BF16)


def _stickbreak(p3, qg, kg, B, S):
    T = B * S
    tq, nf, n_sub = 256, 2, 4
    nq = S // (tq * n_sub)
    npair = BRANCH_W // LANES
    vec = pl.BlockSpec((1, LANES), lambda b, p, i: (0, 0))
    return pl.pallas_call(
        functools.partial(_sb_kernel, S=S, tq=tq, nf=nf, n_sub=n_sub),
        grid=(B, npair, nq),
        in_specs=[pl.BlockSpec((1, tq * n_sub, LANES), lambda b, p, i: (G_SQ + p, b * nq + i, 0)),
                  pl.BlockSpec((1, S, LANES), lambda b, p, i: (G_SK + p, b, 0)),
                  pl.BlockSpec((1, S, LANES), lambda b, p, i: (G_SV + p, b, 0)),
                  vec, vec],
        out_specs=pl.BlockSpec((tq * n_sub, LANES), lambda b, p, i: (b * nq + i, p)),
        out_shape=jax.ShapeDtypeStruct((T, BRANCH_W), BF16),
        scratch_shapes=[pltpu.VMEM((S, LANES), BF16)],
        compiler_params=_params("parallel", "parallel", "arbitrary"),
        name="stickbreak_attn",
    )(p3, p3, p3, qg, kg)


def _ret_kernel(q_ref, k_ref, v_ref, g_ref, ng_ref, o_ref, st_scr, *, ts):
    C = RT_CHUNK

    @pl.when(pl.program_id(1) == 0)
    def _():
        st_scr[...] = jnp.zeros_like(st_scr)

    rel = (lax.broadcasted_iota(I32, (C, C), 0) - lax.broadcasted_iota(I32, (C, C), 1)).astype(F32)
    pos = lax.broadcasted_iota(I32, (C, LANES), 0).astype(F32)
    ng = ng_ref[...]
    scale = RT_DH ** -0.5
    for h in range(RT_HEADS):
        sl = slice(h * LANES, (h + 1) * LANES)
        lg = math.log1p(-(2.0 ** (-5 - h)))
        dm = jnp.where(rel >= 0, jnp.exp(lg * jnp.maximum(rel, 0.0)), 0.0) * scale
        qd = jnp.exp(lg * (pos + 1.0))
        kd = jnp.exp(lg * (C - 1.0 - pos)) * scale
        cd = math.exp(lg * C)
        for c in range(ts // C):
            rs = slice(c * C, (c + 1) * C)
            q = q_ref[h, rs, :]
            k = k_ref[h, rs, :]
            v = v_ref[h, rs, :]
            intra = _dot((_dot_nt(q, k) * dm).astype(BF16), v)
            st = st_scr[h]
            inter = _dot((q.astype(F32) * qd).astype(BF16), st.astype(BF16))
            st_scr[h] = cd * st + _dot_tn((k.astype(F32) * kd).astype(BF16), v)
            o = intra + inter
            o = o * lax.rsqrt(jnp.mean(o * o, axis=-1, keepdims=True) + EPS) * ng[:, sl]
            gg = g_ref[h, rs, :].astype(F32)
            o_ref[rs, sl] = (o * _silu(gg)).astype(BF16)


def _retention(p3, ng, B, S):
    T = B * S
    ts = 512
    nt = S // ts

    def spec(g0):
        return pl.BlockSpec((RT_HEADS, ts, LANES), lambda b, i: (g0 // RT_HEADS, b * nt + i, 0))

    return pl.pallas_call(
        functools.partial(_ret_kernel, ts=ts),
        grid=(B, nt),
        in_specs=[spec(G_RQ), spec(G_RK), spec(G_RV), spec(G_RG),
                  pl.BlockSpec((1, RT_HEADS * RT_DH), lambda b, i: (0, 0))],
        out_specs=pl.BlockSpec((ts, BRANCH_W), lambda b, i: (b * nt + i, 0)),
        out_shape=jax.ShapeDtypeStruct((T, BRANCH_W), BF16),
        scratch_shapes=[pltpu.VMEM((RT_HEADS, RT_DH, RT_DH), F32)],
        compiler_params=_params("parallel", "arbitrary"),
        name="retention",
    )(p3, p3, p3, p3, ng)


def _merge_kernel(x_ref, o0, o1, o2, o3, ng_ref, sc_ref, sh_ref, ga_ref, wg_ref, bg_ref, wb_ref, wo_ref, o_ref):
    D = D_MODEL
    x = x_ref[...]
    hb = _rms_mod(x, ng_ref[...], sc_ref[0], sh_ref[0]).astype(BF16)
    merged = None
    for n, oref in enumerate((o0, o1, o2, o3)):
        gate = jax.nn.sigmoid(_dot(hb, wg_ref[0, :, n * D:(n + 1) * D]) + bg_ref[:, n * D:(n + 1) * D])
        y = _dot(oref[...], wb_ref[0, n * BRANCH_W:(n + 1) * BRANCH_W, :])
        merged = gate * y if merged is None else merged + gate * y
    out = _dot(merged.astype(BF16), wo_ref[0])
    o_ref[...] = x + ga_ref[0] * out


def _merge(x2, outs, ng, sc, sh, ga, w_gate_b, b_gate, w_branch_b, w_out_b, l, S):
    T, D = x2.shape
    tm = 512
    tpb = S // tm
    mod = pl.BlockSpec((1, 1, D), lambda i: (i // tpb, 0, 0))
    osp = pl.BlockSpec((tm, BRANCH_W), lambda i: (i, 0))
    once = dict(pipeline_mode=pl.Buffered(1))
    return pl.pallas_call(
        _merge_kernel,
        grid=(T // tm,),
        in_specs=[pl.BlockSpec((tm, D), lambda i: (i, 0)), osp, osp, osp, osp,
                  pl.BlockSpec((1, D), lambda i: (0, 0)), mod, mod, mod,
                  pl.BlockSpec((1, D, N_BRANCH * D), lambda i: (l, 0, 0), **once),
                  pl.BlockSpec((1, N_BRANCH * D), lambda i: (0, 0)),
                  pl.BlockSpec((1, N_BRANCH * BRANCH_W, D), lambda i: (l, 0, 0), **once),
                  pl.BlockSpec((1, D, D), lambda i: (l, 0, 0), **once)],
        out_specs=pl.BlockSpec((tm, D), lambda i: (i, 0)),
        out_shape=jax.ShapeDtypeStruct((T, D), F32),
        compiler_params=_params("parallel"),
        name="branch_merge",
    )(x2, *outs, ng, sc, sh, ga, w_gate_b, b_gate, w_branch_b, w_out_b)


def _route_kernel(x_ref, ng_ref, sc_ref, sh_ref, w_ref, b_ref, h_ref, info_ref, tcnt_ref, tbase_ref, cnt_ref,
                  carry_scr, *, tm):
    @pl.when(pl.program_id(0) == 0)
    def _():
        carry_scr[...] = jnp.zeros_like(carry_scr)

    h = _rms_mod(x_ref[...], ng_ref[...], sc_ref[0], sh_ref[0])
    h_ref[...] = h
    w = w_ref[...]
    h_hi, h_mid, _ = _split3(h)
    w_hi, w_mid, _ = _split3(w)
    lg = _dot(h_hi, w_hi) + _dot(h_mid, w_hi) + _dot(h_hi, w_mid) + b_ref[...]

    lane = lax.broadcasted_iota(I32, (tm, LANES), 1)
    lane_f = lane.astype(F32)
    neg = -jnp.inf
    big = float(LANES)

    def first_argmax(vals):
        m = jnp.max(vals, axis=-1, keepdims=True)
        idx = jnp.min(jnp.where(vals == m, lane_f, big), axis=-1, keepdims=True)
        return m, idx

    gmask = lane < N_GROUPS
    gl = jnp.where(gmask, lg, neg)
    gmax, gidx = first_argmax(gl)
    gsum = jnp.sum(jnp.where(gmask, jnp.exp(lg - gmax), 0.0), axis=-1, keepdims=True)
    g_p = 1.0 / gsum
    lo = N_GROUPS + EXP_PER_GROUP * gidx
    emask = (lane_f >= lo) & (lane_f < lo + EXP_PER_GROUP)
    el = jnp.where(emask, lg, neg)
    v1, i1 = first_argmax(el)
    el2 = jnp.where(lane_f == i1, neg, el)
    v2, i2 = first_argmax(el2)
    e2x = jnp.exp(v2 - v1)
    p1 = 1.0 / (1.0 + e2x)
    p2 = e2x * p1
    e1 = i1 - N_GROUPS
    e2 = i2 - N_GROUPS

    onehot = jnp.where((lane_f == e1) | (lane_f == e2), 1.0, 0.0)
    rr = lax.broadcasted_iota(I32, (tm, tm), 0)
    cc = lax.broadcasted_iota(I32, (tm, tm), 1)
    before = jnp.where(cc < rr, 1.0, 0.0).astype(BF16)
    local_rank = _dot(before, onehot.astype(BF16))
    tile_cnt = jnp.sum(onehot, axis=0, keepdims=True)
    run_len = jnp.floor((tile_cnt + (DMA_ROWS - 1)) * (1.0 / DMA_ROWS)) * DMA_ROWS
    er = lax.broadcasted_iota(I32, (LANES, LANES), 0)
    ec = lax.broadcasted_iota(I32, (LANES, LANES), 1)
    lower_e = jnp.where(er < ec, 1.0, 0.0).astype(BF16)
    run_start = _dot(jnp.broadcast_to(run_len, (SUB, LANES)).astype(BF16), lower_e)[0:1, :]
    slot = run_start + local_rank
    pos1 = jnp.sum(jnp.where(lane_f == e1, slot, 0.0), axis=-1, keepdims=True)
    pos2 = jnp.sum(jnp.where(lane_f == e2, slot, 0.0), axis=-1, keepdims=True)

    base = carry_scr[0:1, :]
    total = base + run_len
    carry_scr[...] = jnp.broadcast_to(total, carry_scr.shape)
    cnt_ref[...] = jnp.broadcast_to(total, cnt_ref.shape)
    tcnt_ref[0] = jnp.broadcast_to(run_len * (1.0 / DMA_ROWS), (SUB, LANES))
    tbase_ref[0] = jnp.broadcast_to(base, (SUB, LANES))

    info = jnp.where(lane == 0, e1, 0.0)
    info = jnp.where(lane == 1, e2, info)
    info = jnp.where(lane == 2, g_p * p1, info)
    info = jnp.where(lane == 3, g_p * p2, info)
    info = jnp.where(lane == 4, pos1, info)
    info = jnp.where(lane == 5, pos2, info)
    info_ref[...] = info


def _route(x2, ng, sc, sh, w_r, b_r, S):
    T, D = x2.shape
    tm = MOE_TILE
    tpb = S // tm
    nt = T // tm
    mod = pl.BlockSpec((1, 1, D), lambda i: (i // tpb, 0, 0))
    stat = pl.BlockSpec((1, SUB, LANES), lambda i: (i, 0, 0))
    return pl.pallas_call(
        functools.partial(_route_kernel, tm=tm),
        grid=(nt,),
        in_specs=[pl.BlockSpec((tm, D), lambda i: (i, 0)),
                  pl.BlockSpec((1, D), lambda i: (0, 0)), mod, mod,
                  pl.BlockSpec((D, LANES), lambda i: (0, 0)),
                  pl.BlockSpec((1, LANES), lambda i: (0, 0))],
        out_specs=[pl.BlockSpec((tm, D), lambda i: (i, 0)),
                   pl.BlockSpec((tm, LANES), lambda i: (i, 0)),
                   stat, stat,
                   pl.BlockSpec((SUB, LANES), lambda i: (0, 0))],
        out_shape=[jax.ShapeDtypeStruct((T, D), F32),
                   jax.ShapeDtypeStruct((T, LANES), F32),
                   jax.ShapeDtypeStruct((nt, SUB, LANES), F32),
                   jax.ShapeDtypeStruct((nt, SUB, LANES), F32),
                   jax.ShapeDtypeStruct((SUB, LANES), F32)],
        scratch_shapes=[pltpu.VMEM((SUB, LANES), F32)],
        compiler_params=_params("arbitrary"),
        name="moe_route",
    )(x2, ng, sc, sh, w_r, b_r)


def _start_runs(nch_ref, row_ref, tile, make_copy):
    def per_expert(e, issued):
        nch = nch_ref[tile * N_EXPERTS + e]
        row = row_ref[tile * N_EXPERTS + e]

        def issue(c, z):
            make_copy(pl.multiple_of((issued + c) * DMA_ROWS, DMA_ROWS),
                      pl.multiple_of(row + c * DMA_ROWS, DMA_ROWS)).start()
            return z

        lax.fori_loop(0, nch, issue, 0)
        return issued + nch

    return lax.fori_loop(0, N_EXPERTS, per_expert, jnp.int32(0))


def _wait_runs(count, make_copy):
    lax.fori_loop(0, count, lambda c, z: (make_copy(0, 0).wait(), z)[1], 0)


def _dispatch_kernel(nch_ref, row_ref, tail_n_ref, tail_row_ref, used_ref, h_ref, info_ref, xb_ref, srt, zero, sems,
                     pending, *, tm, n_slot):
    i = pl.program_id(0)
    cur = lax.rem(i, 2)
    sem = sems.at[0]

    def copy_zero(row):
        return pltpu.make_async_copy(zero.at[pl.ds(0, DMA_ROWS), :],
                                     xb_ref.at[pl.ds(pl.multiple_of(row, DMA_ROWS), DMA_ROWS), :], sem)

    def copy_zero_block(blk):
        return pltpu.make_async_copy(zero, xb_ref.at[pl.ds(pl.multiple_of(blk * MOE_ROWS, MOE_ROWS), MOE_ROWS), :], sem)

    @pl.when(i == 0)
    def _():
        zero[...] = jnp.zeros_like(zero)

        def per_expert(e, issued):
            lax.fori_loop(0, tail_n_ref[e], lambda c, z: (copy_zero(tail_row_ref[e] + c * DMA_ROWS).start(), z)[1], 0)
            return issued + tail_n_ref[e]

        issued = lax.fori_loop(0, N_EXPERTS, per_expert, jnp.int32(0))
        lax.fori_loop(0, issued, lambda c, z: (copy_zero(0).wait(), z)[1], 0)
        n_blocks = xb_ref.shape[0] // MOE_ROWS
        lax.fori_loop(used_ref[0], n_blocks, lambda b, z: (copy_zero_block(b).start(), z)[1], 0)
        lax.fori_loop(used_ref[0], n_blocks, lambda b, z: (copy_zero_block(0).wait(), z)[1], 0)

    info_t = info_ref[...].T
    slot = lax.broadcasted_iota(I32, (n_slot, tm), 0).astype(F32)
    perm = jnp.where((slot == info_t[4:5, :]) | (slot == info_t[5:6, :]), 1.0, 0.0).astype(BF16)
    srt[cur] = _dot(perm, h_ref[...].astype(BF16))

    def copier(buf):
        def copy(slot_row, buf_row):
            return pltpu.make_async_copy(srt.at[buf, pl.ds(slot_row, DMA_ROWS), :],
                                         xb_ref.at[pl.ds(buf_row, DMA_ROWS), :], sems.at[buf])
        return copy

    @pl.when(i > 0)
    def _():
        _wait_runs(pending[0], copier(1 - cur))

    pending[0] = _start_runs(nch_ref, row_ref, i, copier(cur))

    @pl.when(i == pl.num_programs(0) - 1)
    def _():
        _wait_runs(pending[0], copier(cur))


def _dispatch(h2, info, nch_tab, row_tab, tail_n, tail_row, n_used, n_rows):
    T, D = h2.shape
    tm = MOE_TILE
    n_slot = 2 * tm + N_EXPERTS * DMA_ROWS
    grid_spec = pltpu.PrefetchScalarGridSpec(
        num_scalar_prefetch=5,
        grid=(T // tm,),
        in_specs=[pl.BlockSpec((tm, D), lambda i, *_: (i, 0)),
                  pl.BlockSpec((tm, LANES), lambda i, *_: (i, 0))],
        out_specs=pl.BlockSpec(memory_space=pl.ANY),
        scratch_shapes=[pltpu.VMEM((2, n_slot, D), F32), pltpu.VMEM((MOE_ROWS, D), F32),
                        pltpu.SemaphoreType.DMA((2,)), pltpu.SMEM((1,), I32)],
    )
    return pl.pallas_call(
        functools.partial(_dispatch_kernel, tm=tm, n_slot=n_slot),
        grid_spec=grid_spec,
        out_shape=jax.ShapeDtypeStruct((n_rows, D), F32),
        compiler_params=_params("arbitrary"),
        name="moe_dispatch",
    )(nch_tab, row_tab, tail_n, tail_row, n_used, h2, info)


def _expert_kernel(be_ref, used_ref, x_ref, w1_ref, w3_ref, w2_ref, y_ref, w1_b, w3_b, w2_b):
    i = pl.program_id(0)
    live = i < used_ref[0]

    @pl.when(live & ((i == 0) | (be_ref[i] != be_ref[jnp.maximum(i - 1, 0)])))
    def _():
        w1_b[...] = w1_ref[0, 0].astype(BF16)
        w3_b[...] = w3_ref[0, 0].astype(BF16)
        w2_b[...] = w2_ref[0, 0].astype(BF16)

    @pl.when(live)
    def _():
        x = x_ref[...].astype(BF16)
        a = _silu(_dot(x, w1_b[...])) * _dot(x, w3_b[...])
        y_ref[...] = _dot(a.astype(BF16), w2_b[...])

    @pl.when(jnp.logical_not(live))
    def _():
        y_ref[...] = jnp.zeros_like(y_ref)


def _experts(xb, blk_e, n_used, w1, w3, w2, l):
    P, D = xb.shape
    R = MOE_ROWS
    grid_spec = pltpu.PrefetchScalarGridSpec(
        num_scalar_prefetch=2,
        grid=(P // R,),
        in_specs=[pl.BlockSpec((R, D), lambda i, be, nu: (i, 0)),
                  pl.BlockSpec((1, 1, D, D_EXPERT), lambda i, be, nu: (l, be[i], 0, 0)),
                  pl.BlockSpec((1, 1, D, D_EXPERT), lambda i, be, nu: (l, be[i], 0, 0)),
                  pl.BlockSpec((1, 1, D_EXPERT, D), lambda i, be, nu: (l, be[i], 0, 0))],
        out_specs=pl.BlockSpec((R, D), lambda i, be, nu: (i, 0)),
        scratch_shapes=[pltpu.VMEM((D, D_EXPERT), BF16), pltpu.VMEM((D, D_EXPERT), BF16),
                        pltpu.VMEM((D_EXPERT, D), BF16)],
    )
    return pl.pallas_call(
        _expert_kernel,
        grid_spec=grid_spec,
        out_shape=jax.ShapeDtypeStruct((P, D), F32),
        compiler_params=_params("arbitrary"),
        name="moe_experts",
    )(blk_e, n_used, xb, w1, w3, w2)


def _combine_kernel(nch_ref, row_ref, x_ref, info_ref, ga_ref, yb_ref, o_ref, srt, sems, pending, *, tm, n_slot):
    i = pl.program_id(0)
    cur = lax.rem(i, 2)

    def copier(buf):
        def copy(slot_row, buf_row):
            return pltpu.make_async_copy(yb_ref.at[pl.ds(buf_row, DMA_ROWS), :],
                                         srt.at[buf, pl.ds(slot_row, DMA_ROWS), :], sems.at[buf])
        return copy

    @pl.when(i == 0)
    def _():
        pending[0] = _start_runs(nch_ref, row_ref, 0, copier(0))

    @pl.when(i + 1 < pl.num_programs(0))
    def _():
        pending[1 - cur] = _start_runs(nch_ref, row_ref, i + 1, copier(1 - cur))

    covered = pending[cur]
    _wait_runs(covered, copier(cur))

    def clear(r, z):
        srt[cur, pl.ds(pl.multiple_of(r * DMA_ROWS, DMA_ROWS), DMA_ROWS), :] = jnp.zeros((DMA_ROWS, srt.shape[2]), F32)
        return z

    lax.fori_loop(covered, n_slot // DMA_ROWS, clear, 0)

    info = info_ref[...]
    lane = lax.broadcasted_iota(I32, (tm, LANES), 1)

    def col(k):
        return jnp.sum(jnp.where(lane == k, info, 0.0), axis=-1, keepdims=True)

    slot = lax.broadcasted_iota(I32, (tm, n_slot), 1).astype(F32)
    gates = jnp.where(slot == col(4), col(2), 0.0) + jnp.where(slot == col(5), col(3), 0.0)
    mixed = _dot(gates.astype(BF16), srt[cur].astype(BF16))
    o_ref[...] = x_ref[...] + ga_ref[0] * mixed


def _combine(x2, info, nch_tab, row_tab, ga, yb, S):
    T, D = x2.shape
    tm = MOE_TILE
    tpb = S // tm
    n_slot = 2 * tm + N_EXPERTS * DMA_ROWS
    grid_spec = pltpu.PrefetchScalarGridSpec(
        num_scalar_prefetch=2,
        grid=(T // tm,),
        in_specs=[pl.BlockSpec((tm, D), lambda i, c, d: (i, 0)),
                  pl.BlockSpec((tm, LANES), lambda i, c, d: (i, 0)),
                  pl.BlockSpec((1, 1, D), lambda i, c, d: (i // tpb, 0, 0)),
                  pl.BlockSpec(memory_space=pl.ANY)],
        out_specs=pl.BlockSpec((tm, D), lambda i, c, d: (i, 0)),
        scratch_shapes=[pltpu.VMEM((2, n_slot, D), F32), pltpu.SemaphoreType.DMA((2,)), pltpu.SMEM((2,), I32)],
    )
    return pl.pallas_call(
        functools.partial(_combine_kernel, tm=tm, n_slot=n_slot),
        grid_spec=grid_spec,
        out_shape=jax.ShapeDtypeStruct((T, D), F32),
        compiler_params=_params("arbitrary"),
        name="moe_combine",
    )(nch_tab, row_tab, x2, info, ga, yb)


def _moe(x2, ng, sc, sh, ga, w_r, b_r, w1, w3, w2, l, S):
    T, D = x2.shape
    R = MOE_ROWS
    nt = T // MOE_TILE
    n_rows = (T * 2 + N_EXPERTS * (nt * (DMA_ROWS - 1) + R - 1) + R - 1) // R * R
    h2, info, tnch, tbase, tot = _route(x2, ng, sc, sh, w_r, b_r, S)
    owned = tot[0, :N_EXPERTS].astype(I32)
    padded = (owned + R - 1) // R * R
    pad_end = jnp.cumsum(padded)
    pad_start = pad_end - padded
    nch_tab = tnch[:, 0, :N_EXPERTS].astype(I32).reshape(-1)
    row_tab = (tbase[:, 0, :N_EXPERTS].astype(I32) + pad_start[None, :]).reshape(-1)
    tail_n = (padded - owned) // DMA_ROWS
    tail_row = pad_start + owned
    blk_start = jnp.arange(n_rows // R, dtype=I32) * R
    blk_e = jnp.minimum(jnp.sum((pad_end[None, :] <= blk_start[:, None]).astype(I32), axis=1), N_EXPERTS - 1)
    n_used = (pad_end[-1:] // R).astype(I32)
    xb = _dispatch(h2, info, nch_tab, row_tab, tail_n, tail_row, n_used, n_rows)
    yb = _experts(xb, blk_e, n_used, w1, w3, w2, l)
    return _combine(x2, info, nch_tab, row_tab, ga, yb, S)


def kernel(x, c, ada_w, ada_b, norm1_g, norm2_g, w_in, hgrn_lb, hgrn_norm_g, conv_w, conv_b, conv_ln_g,
           conv_ln_b, sb_qnorm_g, sb_knorm_g, ret_norm_g, w_branch, w_gate, b_gate, w_out, router_group_w,
           router_group_b, router_expert_w, router_expert_b, expert_w1, expert_w3, expert_w2):
    B, S, D = x.shape
    L = ada_w.shape[0]
    T = B * S
    sm = jax.nn.softmax(hgrn_lb.astype(F32), axis=0)
    lower_bounds = jnp.cumsum(sm, axis=0) - sm[0]
    mods = _mods(c, ada_w, ada_b).reshape(L, B, 6, 1, D)
    w_in_b = w_in.astype(BF16)
    w_gate_b = w_gate.astype(BF16)
    w_branch_b = w_branch.astype(BF16)
    w_out_b = w_out.astype(BF16)
    n_r = N_GROUPS + N_EXPERTS
    w_r = jnp.zeros((L, D, LANES), F32).at[:, :, :N_GROUPS].set(router_group_w).at[:, :, N_GROUPS:n_r].set(router_expert_w)
    b_r = jnp.zeros((L, 1, LANES), F32).at[:, 0, :N_GROUPS].set(router_group_b).at[:, 0, N_GROUPS:n_r].set(router_expert_b)

    x2 = x.reshape(T, D)
    for l in range(L):
        sh1, sc1, g1, sh2, sc2, g2 = (mods[l, :, k] for k in range(6))
        n1 = norm1_g[l][None, :]
        p3 = _in_proj(x2, n1, sc1, sh1, w_in_b, l, S)
        o_hg = _hgrn(p3, lower_bounds[l][None, :], hgrn_norm_g[l][None, :], B, S)
        o_cv = _conv(p3, conv_w[l], conv_b[l][None, :], conv_ln_g[l][None, :], conv_ln_b[l][None, :], B, S)
        qg = jnp.tile(sb_qnorm_g[l], 2)[None, :]
        kg = jnp.tile(sb_knorm_g[l], 2)[None, :]
        o_sb = _stickbreak(p3, qg, kg, B, S)
        o_rt = _retention(p3, ret_norm_g[l][None, :], B, S)
        x2 = _merge(x2, (o_hg, o_cv, o_sb, o_rt), n1, sc1, sh1, g1, w_gate_b, b_gate[l][None, :],
                    w_branch_b, w_out_b, l, S)
        x2 = _moe(x2, norm2_g[l][None, :], sc2, sh2, g2, w_r[l], b_r[l], expert_w1, expert_w3, expert_w2, l, S)
    return x2.reshape(B, S, D)
```
